```python
import math
import jax, jax.numpy as jnp
from jax import lax
import numpy as np

D_MODEL = 1024
BATCH = 2
SEQ = 8192
DEPTH = 2

N_MIXERS = 2
N_SB_LAYERS = (DEPTH + 1) // 2
N_MLA_LAYERS = DEPTH // 2
Q_BLOCK = 128
EPS = 1e-6

SB_HEADS = 16
SB_HEAD_DIM = D_MODEL // SB_HEADS

MLA_HEADS = 16
MLA_Q_LORA = 256
MLA_KV_LORA = 128
MLA_NOPE_DIM = 64
MLA_ROPE_DIM = 32
MLA_V_DIM = 64
ROPE_THETA = 10000.0

D_FF = 2816
FFN_RESIDUAL_WEIGHT = 0.5
N_NORMS_PER_LAYER = 6

kernel_name = "hybrid_stickbreaking_mla_macaron"


def rms_norm(x, g):
    xf = x.astype(jnp.float32)
    y = xf * lax.rsqrt(jnp.mean(xf * xf, axis=-1, keepdims=True) + EPS)
    return (y * g.astype(jnp.float32)).astype(x.dtype)


def swiglu(x, w_in, w_out):
    gate, up = jnp.split(x @ w_in, 2, axis=-1)
    return (jax.nn.silu(gate) * up) @ w_out


def apply_rope(x, positions):
    d = x.shape[-1]
    inv_freq = ROPE_THETA ** (-jnp.arange(0, d, 2, dtype=jnp.float32) / d)
    ang = positions.astype(jnp.float32)[..., None] * inv_freq
    cos = jnp.cos(ang)[:, :, None, :].astype(x.dtype)
    sin = jnp.sin(ang)[:, :, None, :].astype(x.dtype)
    x1, x2 = jnp.split(x, 2, axis=-1)
    return jnp.concatenate([x1 * cos - x2 * sin, x2 * cos + x1 * sin], axis=-1)


def query_blocks(q):
    b, s, h, d = q.shape
    return q.reshape(b, s // Q_BLOCK, Q_BLOCK, h, d).transpose(1, 0, 2, 3, 4)


def merge_blocks(o):
    nb, b, qb, h, dv = o.shape
    return o.transpose(1, 0, 2, 3, 4).reshape(b, nb * qb, h * dv)


def stick_breaking_attention(h, w_in, w_out):
    b, s, _ = h.shape
    qkv = (h @ w_in).reshape(b, s, 3, SB_HEADS, SB_HEAD_DIM)
    q, k, v = qkv[:, :, 0], qkv[:, :, 1], qkv[:, :, 2]
    scale = 1.0 / math.sqrt(SB_HEAD_DIM)
    k_idx = jnp.arange(s)

    def block(args):
        qb, bi = args
        q_idx = bi * Q_BLOCK + jnp.arange(Q_BLOCK)
        z = jnp.einsum('bqhd,bkhd->bhqk', qb, k).astype(jnp.float32) * scale
        mask = k_idx[None, :] < q_idx[:, None]
        sp = jnp.where(mask, jax.nn.softplus(z), 0.0)
        tail = lax.cumsum(sp, axis=sp.ndim - 1, reverse=True) - sp
        log_a = jax.nn.log_sigmoid(z) - tail
        a = jnp.where(mask, jnp.exp(log_a), 0.0).astype(v.dtype)
        return jnp.einsum('bhqk,bkhd->bqhd', a, v)

    nb = s // Q_BLOCK
    o = lax.map(block, (query_blocks(q), jnp.arange(nb)))
    return merge_blocks(o) @ w_out


def multi_head_latent_attention(h, positions, w_in, q_norm, w_uq, kv_norm, w_ukv, w_out):
    b, s, _ = h.shape
    proj = h @ w_in
    c_q = proj[..., :MLA_Q_LORA]
    c_kv = proj[..., MLA_Q_LORA:MLA_Q_LORA + MLA_KV_LORA]
    k_rope = proj[..., MLA_Q_LORA + MLA_KV_LORA:][:, :, None, :]
    k_rope = apply_rope(k_rope, positions)[:, :, 0, :]

    q = (rms_norm(c_q, q_norm) @ w_uq).reshape(b, s, MLA_HEADS, MLA_NOPE_DIM + MLA_ROPE_DIM)
    q_nope = q[..., :MLA_NOPE_DIM]
    q_rope = apply_rope(q[..., MLA_NOPE_DIM:], positions)
    q = jnp.concatenate([q_nope, q_rope], axis=-1)

    kv = (rms_norm(c_kv, kv_norm) @ w_ukv).reshape(b, s, MLA_HEADS, MLA_NOPE_DIM + MLA_V_DIM)
    k_nope = kv[..., :MLA_NOPE_DIM]
    v = kv[..., MLA_NOPE_DIM:]
    scale = 1.0 / math.sqrt(MLA_NOPE_DIM + MLA_ROPE_DIM)
    k_idx = jnp.arange(s)

    def block(args):
        qb, bi = args
        q_idx = bi * Q_BLOCK + jnp.arange(Q_BLOCK)
        qn, qr = qb[..., :MLA_NOPE_DIM], qb[..., MLA_NOPE_DIM:]
        scores = (jnp.einsum('bqhd,bkhd->bhqk', qn, k_nope)
                  + jnp.einsum('bqhr,bkr->bhqk', qr, k_rope)).astype(jnp.float32) * scale
        mask = k_idx[None, :] <= q_idx[:, None]
        scores = jnp.where(mask, scores, -jnp.inf)
        p = jax.nn.softmax(scores, axis=-1).astype(v.dtype)
        return jnp.einsum('bhqk,bkhd->bqhd', p, v)

    nb = s // Q_BLOCK
    o = lax.map(block, (query_blocks(q), jnp.arange(nb)))
    return merge_blocks(o) @ w_out


def setup_inputs(seed: int = 0) -> dict:
    key = jax.random.key(seed)
    ks = jax.random.split(key, 16)

    def w(k, shape, fan_in):
        return jax.random.normal(k, shape, jnp.float32) * fan_in ** -0.5

    def gain(k, shape):
        return 1.0 + 0.05 * jax.random.normal(k, shape, jnp.float32)

    sb_width = SB_HEADS * SB_HEAD_DIM
    mla_in = MLA_Q_LORA + MLA_KV_LORA + MLA_ROPE_DIM
    return {
        "x": jax.random.normal(ks[0], (BATCH, SEQ, D_MODEL), jnp.float32),
        "positions": jnp.broadcast_to(jnp.arange(SEQ, dtype=jnp.int32)[None, :], (BATCH, SEQ)),
        "norm_g": gain(ks[1], (DEPTH, N_NORMS_PER_LAYER, D_MODEL)),
        "ffn_w_in": w(ks[2], (DEPTH, 2, D_MODEL, 2 * D_FF), D_MODEL),
        "ffn_w_out": w(ks[3], (DEPTH, 2, D_FF, D_MODEL), D_FF),
        "sb_w_in": w(ks[4], (N_SB_LAYERS, D_MODEL, 3 * sb_width), D_MODEL),
        "sb_w_out": w(ks[5], (N_SB_LAYERS, sb_width, D_MODEL), sb_width),
        "mla_w_in": w(ks[6], (N_MLA_LAYERS, D_MODEL, mla_in), D_MODEL),
        "mla_q_norm": gain(ks[7], (N_MLA_LAYERS, MLA_Q_LORA)),
        "mla_w_uq": w(ks[8], (N_MLA_LAYERS, MLA_Q_LORA, MLA_HEADS * (MLA_NOPE_DIM + MLA_ROPE_DIM)), MLA_Q_LORA),
        "mla_kv_norm": gain(ks[9], (N_MLA_LAYERS, MLA_KV_LORA)),
        "mla_w_ukv": w(ks[10], (N_MLA_LAYERS, MLA_KV_LORA, MLA_HEADS * (MLA_NOPE_DIM + MLA_V_DIM)), MLA_KV_LORA),
        "mla_w_out": w(ks[11], (N_MLA_LAYERS, MLA_HEADS * MLA_V_DIM, D_MODEL), MLA_HEADS * MLA_V_DIM),
    }


def reference(x, positions, norm_g, ffn_w_in, ffn_w_out, sb_w_in, sb_w_out,
              mla_w_in, mla_q_norm, mla_w_uq, mla_kv_norm, mla_w_ukv, mla_w_out):
    for i in range(DEPTH):
        g = norm_g[i]
        f = swiglu(rms_norm(x, g[0]), ffn_w_in[i, 0], ffn_w_out[i, 0])
        x = x + FFN_RESIDUAL_WEIGHT * rms_norm(f, g[1])
        h = rms_norm(x, g[2])
        j = i // N_MIXERS
        if i % N_MIXERS == 0:
            m = stick_breaking_attention(h, sb_w_in[j], sb_w_out[j])
        else:
            m = multi_head_latent_attention(h, positions, mla_w_in[j], mla_q_norm[j], mla_w_uq[j],
                                            mla_kv_norm[j], mla_w_ukv[j], mla_w_out[j])
        x = x + rms_norm(m, g[3])
        f = swiglu(rms_norm(x, g[4]), ffn_w_in[i, 1], ffn_w_out[i, 1])
        x = x + FFN_RESIDUAL_WEIGHT * rms_norm(f, g[5])
    return x
```

```python
import functools
import math

import jax
import jax.numpy as jnp
from jax import lax
from jax.experimental import pallas as pl
from jax.experimental.pallas import tpu as pltpu

D_MODEL = 1024
EPS = 1e-6
D_FF = 2816
FFN_RESIDUAL_WEIGHT = 0.5

SB_HEADS = 16
SB_HEAD_DIM = 64

MLA_HEADS = 16
MLA_Q_LORA = 256
MLA_KV_LORA = 128
MLA_NOPE_DIM = 64
MLA_ROPE_DIM = 32
MLA_V_DIM = 64
ROPE_THETA = 10000.0

LANES = 128
HEAD_SLAB = 128
HEADS_PER_STEP = 2
VMEM_LIMIT_BYTES = 56 * 1024 * 1024

SB_TAIL_CUTOFF = 104.0
SB_BLOCK = 128
SB_STATIC_BLOCKS = 3

MLA_BLOCK = 256


def _rms(x, g):
    return x * lax.rsqrt(jnp.mean(x * x, axis=-1, keepdims=True) + EPS) * g


def _dot(a, b):
    return jnp.dot(a, b, preferred_element_type=jnp.float32)


def _dot_nt(a, b):
    return lax.dot_general(a, b, (((1,), (1,)), ((), ())),
                           preferred_element_type=jnp.float32)


def _params(semantics):
    return pltpu.CompilerParams(dimension_semantics=semantics,
                                vmem_limit_bytes=VMEM_LIMIT_BYTES)


def _const_spec(shape):
    return pl.BlockSpec(shape, lambda *_: (0,) * len(shape),
                        pipeline_mode=pl.Buffered(1))


def _ffn_kernel(x_ref, gpre_ref, gpost_ref, win_ref, wout_ref, o_ref):
    x = x_ref[...]
    xn = _rms(x, gpre_ref[...]).astype(jnp.bfloat16)
    h = _dot(xn, win_ref[...])
    gate = h[:, :D_FF]
    up = h[:, D_FF:]
    act = (gate * jax.nn.sigmoid(gate) * up).astype(jnp.bfloat16)
    f = _dot(act, wout_ref[...])
    o_ref[...] = x + FFN_RESIDUAL_WEIGHT * _rms(f, gpost_ref[...])


def _ffn(x, g_pre, g_post, w_in, w_out, tm=256):
    t = x.shape[0]
    return pl.pallas_call(
        _ffn_kernel,
        grid=(t // tm,),
        in_specs=[
            pl.BlockSpec((tm, D_MODEL), lambda i: (i, 0)),
            _const_spec((1, D_MODEL)),
            _const_spec((1, D_MODEL)),
            _const_spec((D_MODEL, 2 * D_FF)),
            _const_spec((D_FF, D_MODEL)),
        ],
        out_specs=pl.BlockSpec((tm, D_MODEL), lambda i: (i, 0)),
        out_shape=jax.ShapeDtypeStruct((t, D_MODEL), jnp.float32),
        compiler_params=_params(("parallel",)),
        name="ffn",
    )(x, g_pre, g_post, w_in, w_out)


def _outproj_kernel(x_ref, o_ref, g_ref, w_ref, y_ref):
    m = _dot(o_ref[...], w_ref[...])
    y_ref[...] = x_ref[...] + _rms(m, g_ref[...])


def _outproj(x, o, g, w, tm=512):
    t = x.shape[0]
    return pl.pallas_call(
        _outproj_kernel,
        grid=(t // tm,),
        in_specs=[
            pl.BlockSpec((tm, D_MODEL), lambda i: (i, 0)),
            pl.BlockSpec((tm, D_MODEL), lambda i: (i, 0)),
            _const_spec((1, D_MODEL)),
            _const_spec((D_MODEL, D_MODEL)),
        ],
        out_specs=pl.BlockSpec((tm, D_MODEL), lambda i: (i, 0)),
        out_shape=jax.ShapeDtypeStruct((t, D_MODEL), jnp.float32),
        compiler_params=_params(("parallel",)),
        name="outproj",
    )(x, o, g, w)


def _sb_qkv_kernel(x_ref, g_ref, w_ref, o_ref):
    h = _rms(x_ref[...], g_ref[...]).astype(jnp.bfloat16)
    y = _dot(h, w_ref[...])
    scale = jnp.where(pl.program_id(1) == 0, 1.0 / math.sqrt(SB_HEAD_DIM), 1.0)
    o_ref[...] = (y * scale).astype(jnp.bfloat16)


def _sb_qkv(x, g, w, tm=512):
    t = x.shape[0]
    width = SB_HEADS * SB_HEAD_DIM
    return pl.pallas_call(
        _sb_qkv_kernel,
        grid=(t // tm, 3),
        in_specs=[
            pl.BlockSpec((tm, D_MODEL), lambda i, j: (i, 0)),
            _const_spec((1, D_MODEL)),
            pl.BlockSpec((D_MODEL, width), lambda i, j: (0, j)),
        ],
        out_specs=pl.BlockSpec((tm, width), lambda i, j: (i, j)),
        out_shape=jax.ShapeDtypeStruct((t, 3 * width), jnp.bfloat16),
        compiler_params=_params(("parallel", "arbitrary")),
        name="sb_qkv",
    )(x, g, w)


def _sb_attn_kernel(q_ref, k_ref, v_ref, o_ref):
    qi = pl.program_id(2)
    blk = SB_BLOCK
    q = q_ref[0]
    lane = lax.broadcasted_iota(jnp.int32, (blk, LANES), 1)
    row = lax.broadcasted_iota(jnp.int32, (blk, blk), 0)
    col = lax.broadcasted_iota(jnp.int32, (blk, blk), 1)
    causal = col < row
    ur = lax.broadcasted_iota(jnp.int32, (2 * blk, blk), 0)
    uc = lax.broadcasted_iota(jnp.int32, (2 * blk, blk), 1)
    cum = jnp.where((ur % blk) > uc, 1.0, 0.0).astype(jnp.bfloat16)

    zero = jnp.zeros_like(q)
    q_heads = [jnp.where((lane // SB_HEAD_DIM) == h, q, zero) for h in range(HEADS_PER_STEP)]

    def block(j, q_h, tail, acc, mask):
        start = pl.multiple_of(j * blk, blk)
        k_blk = k_ref[0, pl.ds(start, blk), :]
        v_blk = v_ref[0, pl.ds(start, blk), :]
        z = _dot_nt(q_h, k_blk)
        sp = jnp.maximum(z, 0.0) + jnp.log(1.0 + jnp.exp(-jnp.abs(z)))
        sp_m = jnp.where(mask, sp, 0.0)
        hi = sp_m.astype(jnp.bfloat16)
        lo = (sp_m - hi.astype(jnp.float32)).astype(jnp.bfloat16)
        inner = _dot(jnp.concatenate([hi, lo], axis=1), cum)
        a = jnp.where(mask, jnp.exp(z - sp - inner - tail), 0.0)
        acc = acc + _dot(a.astype(jnp.bfloat16), v_blk)
        tail = tail + jnp.sum(sp_m, axis=1, keepdims=True)
        return tail, acc

    tails = [jnp.zeros((blk, 1), jnp.float32) for _ in range(HEADS_PER_STEP)]
    accs = [jnp.zeros((blk, LANES), jnp.float32) for _ in range(HEADS_PER_STEP)]
    for d in range(SB_STATIC_BLOCKS):
        j = jnp.maximum(qi - d, 0)
        mask = causal if d == 0 else jnp.broadcast_to(qi >= d, (blk, blk))
        for h in range(HEADS_PER_STEP):
            tails[h], accs[h] = block(j, q_heads[h], tails[h], accs[h], mask)

    def live(state):
        j, t0, _, t1, _ = state
        return jnp.logical_and(j >= 0, jnp.minimum(jnp.min(t0), jnp.min(t1)) < SB_TAIL_CUTOFF)

    def older(state):
        j, t0, a0, t1, a1 = state
        full = jnp.full((blk, blk), True)
        t0, a0 = block(j, q_heads[0], t0, a0, full)
        t1, a1 = block(j, q_heads[1], t1, a1, full)
        return j - 1, t0, a0, t1, a1

    _, _, acc0, _, acc1 = lax.while_loop(
        live, older, (qi - SB_STATIC_BLOCKS, tails[0], accs[0], tails[1], accs[1]))
    o_ref[0] = jnp.where(lane < SB_HEAD_DIM, acc0, acc1).astype(o_ref.dtype)


def _sb_attention(qkv, batch, seq):
    pairs = SB_HEADS // HEADS_PER_STEP
    nq = seq // SB_BLOCK
    return pl.pallas_call(
        _sb_attn_kernel,
        grid=(batch, pairs, nq),
        in_specs=[
            pl.BlockSpec((1, SB_BLOCK, LANES), lambda b, p, i: (b, i, p)),
            pl.BlockSpec((1, seq, LANES), lambda b, p, i: (b, 0, pairs + p)),
            pl.BlockSpec((1, seq, LANES), lambda b, p, i: (b, 0, 2 * pairs + p)),
        ],
        out_specs=pl.BlockSpec((1, SB_BLOCK, LANES), lambda b, p, i: (b, i, p)),
        out_shape=jax.ShapeDtypeStruct((batch, seq, SB_HEADS * SB_HEAD_DIM), jnp.bfloat16),
        compiler_params=_params(("parallel", "parallel", "arbitrary")),
        name="sb_attn",
    )(qkv, qkv, qkv)


def _mla_proj_kernel(x_ref, pos_ref, freq_ref, g_ref, win_ref, qn_ref, kvn_ref,
                     wuq_ref, wuk_ref, wuv_ref, q_ref, k_ref, v_ref):
    h = _rms(x_ref[...], g_ref[...]).astype(jnp.bfloat16)
    proj = _dot(h, win_ref[...])
    c_q = proj[:, :MLA_Q_LORA]
    c_kv = proj[:, MLA_Q_LORA:MLA_Q_LORA + MLA_KV_LORA]
    kr = proj[:, MLA_Q_LORA + MLA_KV_LORA:MLA_Q_LORA + MLA_KV_LORA + HEAD_SLAB]
    kr_rot = proj[:, MLA_Q_LORA + MLA_KV_LORA + HEAD_SLAB:]

    tm = x_ref.shape[0]
    lane = lax.broadcasted_iota(jnp.int32, (tm, HEAD_SLAB), 1)
    ang = pos_ref[...].astype(jnp.float32) * freq_ref[...]
    is_rope = jnp.logical_and(lane >= MLA_NOPE_DIM, lane < MLA_NOPE_DIM + MLA_ROPE_DIM)
    cos = jnp.where(lane < MLA_NOPE_DIM, 1.0, jnp.where(is_rope, jnp.cos(ang), 0.0))
    sin = jnp.where(is_rope, jnp.sin(ang), 0.0)

    qn = _rms(c_q, qn_ref[...]).astype(jnp.bfloat16)
    q_all = _dot(qn, wuq_ref[...])
    kvn = _rms(c_kv, kvn_ref[...]).astype(jnp.bfloat16)
    k_nope = _dot(kvn, wuk_ref[...])
    v_ref[...] = _dot(kvn, wuv_ref[...]).astype(v_ref.dtype)
    k_rope = kr * cos + kr_rot * sin
    width = MLA_HEADS * HEAD_SLAB
    for hd in range(MLA_HEADS):
        sl = slice(hd * HEAD_SLAB, (hd + 1) * HEAD_SLAB)
        rot = slice(width + hd * HEAD_SLAB, width + (hd + 1) * HEAD_SLAB)
        q_ref[:, sl] = (q_all[:, sl] * cos + q_all[:, rot] * sin).astype(q_ref.dtype)
        k_ref[:, sl] = (k_nope[:, sl] + k_rope).astype(k_ref.dtype)


def _mla_weights(w_in, w_uq, w_ukv):
    half = MLA_ROPE_DIM // 2
    pad = HEAD_SLAB - MLA_NOPE_DIM - MLA_ROPE_DIM

    def slab(nope, rope):
        return jnp.concatenate(
            [nope, rope, jnp.zeros(rope.shape[:-1] + (pad,), rope.dtype)], axis=-1)

    def rotate(rope):
        return jnp.concatenate([-rope[..., half:], rope[..., :half]], axis=-1)

    lat = MLA_Q_LORA + MLA_KV_LORA
    kr_w = w_in[:, lat:]
    zeros_in = jnp.zeros((D_MODEL, MLA_NOPE_DIM), w_in.dtype)
    w_in_p = jnp.concatenate(
        [w_in[:, :lat], slab(zeros_in, kr_w), slab(zeros_in, rotate(kr_w))], axis=1)

    uq = w_uq.reshape(MLA_Q_LORA, MLA_HEADS, MLA_NOPE_DIM + MLA_ROPE_DIM)
    uq_nope, uq_rope = uq[..., :MLA_NOPE_DIM], uq[..., MLA_NOPE_DIM:]
    w_uq_p = jnp.concatenate(
        [slab(uq_nope, uq_rope).reshape(MLA_Q_LORA, -1),
         slab(jnp.zeros_like(uq_nope), rotate(uq_rope)).reshape(MLA_Q_LORA, -1)], axis=1)

    ukv = w_ukv.reshape(MLA_KV_LORA, MLA_HEADS, MLA_NOPE_DIM + MLA_V_DIM)
    uk = ukv[..., :MLA_NOPE_DIM]
    w_uk_p = jnp.concatenate([uk, jnp.zeros_like(uk)], axis=-1).reshape(MLA_KV_LORA, -1)
    w_uv = ukv[..., MLA_NOPE_DIM:].reshape(MLA_KV_LORA, -1)
    bf = jnp.bfloat16
    return w_in_p.astype(bf), w_uq_p.astype(bf), w_uk_p.astype(bf), w_uv.astype(bf)


def _rope_freq_slab():
    inv_freq = ROPE_THETA ** (-jnp.arange(0, MLA_ROPE_DIM, 2, dtype=jnp.float32) / MLA_ROPE_DIM)
    zeros = jnp.zeros((MLA_NOPE_DIM,), jnp.float32)
    pad = jnp.zeros((HEAD_SLAB - MLA_NOPE_DIM - MLA_ROPE_DIM,), jnp.float32)
    return jnp.concatenate([zeros, inv_freq, inv_freq, pad])[None, :]


def _mla_proj(x, pos, g, w_in_p, q_norm, kv_norm, w_uq_p, w_uk_p, w_uv, tm=256):
    t = x.shape[0]
    width = MLA_HEADS * HEAD_SLAB
    vwidth = MLA_HEADS * MLA_V_DIM
    row = lambda i: (i, 0)
    return pl.pallas_call(
        _mla_proj_kernel,
        grid=(t // tm,),
        in_specs=[
            pl.BlockSpec((tm, D_MODEL), row),
            pl.BlockSpec((tm, 1), row),
            _const_spec((1, HEAD_SLAB)),
            _const_spec((1, D_MODEL)),
            _const_spec(w_in_p.shape),
            _const_spec((1, MLA_Q_LORA)),
            _const_spec((1, MLA_KV_LORA)),
            _const_spec(w_uq_p.shape),
            _const_spec(w_uk_p.shape),
            _const_spec(w_uv.shape),
        ],
        out_specs=[
            pl.BlockSpec((tm, width), row),
            pl.BlockSpec((tm, width), row),
            pl.BlockSpec((tm, vwidth), row),
        ],
        out_shape=[
            jax.ShapeDtypeStruct((t, width), jnp.bfloat16),
            jax.ShapeDtypeStruct((t, width), jnp.bfloat16),
            jax.ShapeDtypeStruct((t, vwidth), jnp.bfloat16),
        ],
        compiler_params=_params(("parallel",)),
        name="mla_proj",
    )(x, pos, _rope_freq_slab(), g, w_in_p, q_norm, kv_norm, w_uq_p, w_uk_p, w_uv)


def _mla_attn_kernel(q_ref, k_ref, v_ref, o_ref):
    qi = pl.program_id(2)
    blk = MLA_BLOCK
    c = math.log2(math.e) / math.sqrt(MLA_NOPE_DIM + MLA_ROPE_DIM)
    q_heads = [q_ref[0, :, h * HEAD_SLAB:(h + 1) * HEAD_SLAB] for h in range(HEADS_PER_STEP)]
    row = lax.broadcasted_iota(jnp.int32, (blk, blk), 0)
    col = lax.broadcasted_iota(jnp.int32, (blk, blk), 1)
    causal = col <= row

    def step(j, state, masked):
        start = pl.multiple_of(j * blk, blk)
        v_blk = v_ref[0, pl.ds(start, blk), :]
        out = []
        for h in range(HEADS_PER_STEP):
            m, l, acc = state[h]
            k_blk = k_ref[0, pl.ds(start, blk), h * HEAD_SLAB:(h + 1) * HEAD_SLAB]
            s = _dot_nt(q_heads[h], k_blk) * c
            if masked:
                s = jnp.where(causal, s, -jnp.inf)
            m_new = jnp.maximum(m, jnp.max(s, axis=1, keepdims=True))
            alpha = jnp.exp2(m - m_new)
            p = jnp.exp2(s - m_new)
            l = alpha * l + jnp.sum(p, axis=1, keepdims=True)
            acc = alpha * acc + _dot(p.astype(jnp.bfloat16), v_blk)
            out.append((m_new, l, acc))
        return tuple(out)

    init = tuple((jnp.full((blk, 1), -jnp.inf, jnp.float32),
                  jnp.zeros((blk, 1), jnp.float32),
                  jnp.zeros((blk, LANES), jnp.float32)) for _ in range(HEADS_PER_STEP))
    state = lax.fori_loop(0, qi, lambda j, s: step(j, s, False), init)
    state = step(qi, state, True)
    (_, l0, acc0), (_, l1, acc1) = state
    lane = lax.broadcasted_iota(jnp.int32, (blk, LANES), 1)
    o_ref[0] = jnp.where(lane < MLA_V_DIM, acc0 / l0, acc1 / l1).astype(o_ref.dtype)


def _mla_attention(q, k, v, batch, seq):
    pairs = MLA_HEADS // HEADS_PER_STEP
    nq = seq // MLA_BLOCK
    qk_w = HEADS_PER_STEP * HEAD_SLAB
    return pl.pallas_call(
        _mla_attn_kernel,
        grid=(batch, pairs, nq),
        in_specs=[
            pl.BlockSpec((1, MLA_BLOCK, qk_w), lambda b, p, i: (b, i, p)),
            pl.BlockSpec((1, seq, qk_w), lambda b, p, i: (b, 0, p)),
            pl.BlockSpec((1, seq, LANES), lambda b, p, i: (b, 0, p)),
        ],
        out_specs=pl.BlockSpec((1, MLA_BLOCK, LANES), lambda b, p, i: (b, i, p)),
        out_shape=jax.ShapeDtypeStruct((batch, seq, MLA_HEADS * MLA_V_DIM), jnp.bfloat16),
        compiler_params=_params(("parallel", "parallel", "arbitrary")),
        name="mla_attn",
    )(q, k, v)


def kernel(x, positions, norm_g, ffn_w_in, ffn_w_out, sb_w_in, sb_w_out, mla_w_in,
           mla_q_norm, mla_w_uq, mla_kv_norm, mla_w_ukv, mla_w_out):
    batch, seq, _ = x.shape
    t = batch * seq
    bf = jnp.bfloat16
    depth = norm_g.shape[0]
    xt = x.reshape(t, D_MODEL)
    pos = positions.reshape(t, 1)
    for i in range(depth):
        g = norm_g[i][:, None, :]
        xt = _ffn(xt, g[0], g[1], ffn_w_in[i, 0].astype(bf), ffn_w_out[i, 0].astype(bf))
        j = i // 2
        if i % 2 == 0:
            qkv = _sb_qkv(xt, g[2], sb_w_in[j].astype(bf))
            o = _sb_attention(qkv.reshape(batch, seq, -1), batch, seq)
            w_out = sb_w_out[j]
        else:
            w_in_p, w_uq_p, w_uk_p, w_uv = _mla_weights(mla_w_in[j], mla_w_uq[j], mla_w_ukv[j])
            q, k, v = _mla_proj(xt, pos, g[2], w_in_p, mla_q_norm[j][None, :],
                                mla_kv_norm[j][None, :], w_uq_p, w_uk_p, w_uv)
            o = _mla_attention(q.reshape(batch, seq, -1), k.reshape(batch, seq, -1),
                               v.reshape(batch, seq, -1), batch, seq)
            w_out = mla_w_out[j]
        xt = _outproj(xt, o.reshape(t, -1), g[3], w_out.astype(bf))
        xt = _ffn(xt, g[4], g[5], ffn_w_in[i, 1].astype(bf), ffn_w_out[i, 1].astype(bf))
    return xt.reshape(batch, seq, D_MODEL)
```

```python
import functools
import math

import jax
import jax.numpy as jnp
from jax import lax
from jax.experimental import pallas as pl
from jax.experimental.pallas import tpu as pltpu

D_MODEL = 1024
EPS = 1e-6
D_FF = 2816
FFN_RESIDUAL_WEIGHT = 0.5

SB_HEADS = 16
SB_HEAD_DIM = 64

MLA_HEADS = 16
MLA_Q_LORA = 256
MLA_KV_LORA = 128
MLA_NOPE_DIM = 64
MLA_ROPE_DIM = 32
MLA_V_DIM = 64
ROPE_THETA = 10000.0

LANES = 128
HEAD_SLAB = 128
HEADS_PER_STEP = 2
VMEM_LIMIT_BYTES = 56 * 1024 * 1024

SB_TAIL_CUTOFF = 104.0
SB_BLOCK = 128
SB_STATIC_BLOCKS = 3

MLA_Q_BLOCK = 256
MLA_K_BLOCK = 512
MLA_V_CHUNK = 256
MLA_DENOM_ROWS = 16


def _rms(x, g):
    return x * lax.rsqrt(jnp.mean(x * x, axis=-1, keepdims=True) + EPS) * g


def _dot(a, b):
    return jnp.dot(a, b, preferred_element_type=jnp.float32)


def _dot_nt(a, b):
    return lax.dot_general(a, b, (((1,), (1,)), ((), ())),
                           preferred_element_type=jnp.float32)


def _params(semantics):
    return pltpu.CompilerParams(dimension_semantics=semantics,
                                vmem_limit_bytes=VMEM_LIMIT_BYTES)


def _const_spec(shape):
    return pl.BlockSpec(shape, lambda *_: (0,) * len(shape),
                        pipeline_mode=pl.Buffered(1))


def _ffn_kernel(x_ref, gpre_ref, gpost_ref, win_ref, wout_ref, o_ref):
    x = x_ref[...]
    xn = _rms(x, gpre_ref[...]).astype(jnp.bfloat16)
    h = _dot(xn, win_ref[...])
    gate = h[:, :D_FF]
    up = h[:, D_FF:]
    act = (gate * jax.nn.sigmoid(gate) * up).astype(jnp.bfloat16)
    f = _dot(act, wout_ref[...])
    o_ref[...] = x + FFN_RESIDUAL_WEIGHT * _rms(f, gpost_ref[...])


def _ffn(x, g_pre, g_post, w_in, w_out, tm=256):
    t = x.shape[0]
    return pl.pallas_call(
        _ffn_kernel,
        grid=(t // tm,),
        in_specs=[
            pl.BlockSpec((tm, D_MODEL), lambda i: (i, 0)),
            _const_spec((1, D_MODEL)),
            _const_spec((1, D_MODEL)),
            _const_spec((D_MODEL, 2 * D_FF)),
            _const_spec((D_FF, D_MODEL)),
        ],
        out_specs=pl.BlockSpec((tm, D_MODEL), lambda i: (i, 0)),
        out_shape=jax.ShapeDtypeStruct((t, D_MODEL), jnp.float32),
        compiler_params=_params(("parallel",)),
        name="ffn",
    )(x, g_pre, g_post, w_in, w_out)


def _outproj_kernel(x_ref, o_ref, g_ref, w_ref, y_ref):
    m = _dot(o_ref[...], w_ref[...])
    y_ref[...] = x_ref[...] + _rms(m, g_ref[...])


def _outproj(x, o, g, w, tm=512):
    t = x.shape[0]
    return pl.pallas_call(
        _outproj_kernel,
        grid=(t // tm,),
        in_specs=[
            pl.BlockSpec((tm, D_MODEL), lambda i: (i, 0)),
            pl.BlockSpec((tm, D_MODEL), lambda i: (i, 0)),
            _const_spec((1, D_MODEL)),
            _const_spec((D_MODEL, D_MODEL)),
        ],
        out_specs=pl.BlockSpec((tm, D_MODEL), lambda i: (i, 0)),
        out_shape=jax.ShapeDtypeStruct((t, D_MODEL), jnp.float32),
        compiler_params=_params(("parallel",)),
        name="outproj",
    )(x, o, g, w)


def _sb_qkv_kernel(x_ref, g_ref, w_ref, o_ref):
    h = _rms(x_ref[...], g_ref[...]).astype(jnp.bfloat16)
    y = _dot(h, w_ref[...])
    scale = jnp.where(pl.program_id(1) == 0, 1.0 / math.sqrt(SB_HEAD_DIM), 1.0)
    o_ref[...] = (y * scale).astype(jnp.bfloat16)


def _sb_qkv(x, g, w, tm=512):
    t = x.shape[0]
    width = SB_HEADS * SB_HEAD_DIM
    return pl.pallas_call(
        _sb_qkv_kernel,
        grid=(t // tm, 3),
        in_specs=[
            pl.BlockSpec((tm, D_MODEL), lambda i, j: (i, 0)),
            _const_spec((1, D_MODEL)),
            pl.BlockSpec((D_MODEL, width), lambda i, j: (0, j)),
        ],
        out_specs=pl.BlockSpec((tm, width), lambda i, j: (i, j)),
        out_shape=jax.ShapeDtypeStruct((t, 3 * width), jnp.bfloat16),
        compiler_params=_params(("parallel", "arbitrary")),
        name="sb_qkv",
    )(x, g, w)


def _sb_attn_kernel(q_ref, k_ref, v_ref, o_ref):
    qi = pl.program_id(2)
    blk = SB_BLOCK
    q = q_ref[0]
    lane = lax.broadcasted_iota(jnp.int32, (blk, LANES), 1)
    row = lax.broadcasted_iota(jnp.int32, (blk, blk), 0)
    col = lax.broadcasted_iota(jnp.int32, (blk, blk), 1)
    causal = col < row
    ur = lax.broadcasted_iota(jnp.int32, (2 * blk, blk), 0)
    uc = lax.broadcasted_iota(jnp.int32, (2 * blk, blk), 1)
    cum = jnp.where((ur % blk) > uc, 1.0, 0.0).astype(jnp.bfloat16)

    zero = jnp.zeros_like(q)
    q_heads = [jnp.where((lane // SB_HEAD_DIM) == h, q, zero) for h in range(HEADS_PER_STEP)]

    def block(j, q_h, tail, acc, mask):
        start = pl.multiple_of(j * blk, blk)
        k_blk = k_ref[0, pl.ds(start, blk), :]
        v_blk = v_ref[0, pl.ds(start, blk), :]
        z = _dot_nt(q_h, k_blk)
        sp = jnp.maximum(z, 0.0) + jnp.log(1.0 + jnp.exp(-jnp.abs(z)))
        sp_m = jnp.where(mask, sp, 0.0)
        hi = sp_m.astype(jnp.bfloat16)
        lo = (sp_m - hi.astype(jnp.float32)).astype(jnp.bfloat16)
        inner = _dot(jnp.concatenate([hi, lo], axis=1), cum)
        a = jnp.where(mask, jnp.exp(z - sp - inner - tail), 0.0)
        acc = acc + _dot(a.astype(jnp.bfloat16), v_blk)
        tail = tail + jnp.sum(sp_m, axis=1, keepdims=True)
        return tail, acc

    tails = [jnp.zeros((blk, 1), jnp.float32) for _ in range(HEADS_PER_STEP)]
    accs = [jnp.zeros((blk, LANES), jnp.float32) for _ in range(HEADS_PER_STEP)]
    for d in range(SB_STATIC_BLOCKS):
        j = jnp.maximum(qi - d, 0)
        mask = causal if d == 0 else jnp.broadcast_to(qi >= d, (blk, blk))
        for h in range(HEADS_PER_STEP):
            tails[h], accs[h] = block(j, q_heads[h], tails[h], accs[h], mask)

    def live(state):
        j, t0, _, t1, _ = state
        return jnp.logical_and(j >= 0, jnp.minimum(jnp.min(t0), jnp.min(t1)) < SB_TAIL_CUTOFF)

    def older(state):
        j, t0, a0, t1, a1 = state
        full = jnp.full((blk, blk), True)
        t0, a0 = block(j, q_heads[0], t0, a0, full)
        t1, a1 = block(j, q_heads[1], t1, a1, full)
        return j - 1, t0, a0, t1, a1

    _, _, acc0, _, acc1 = lax.while_loop(
        live, older, (qi - SB_STATIC_BLOCKS, tails[0], accs[0], tails[1], accs[1]))
    o_ref[0] = jnp.where(lane < SB_HEAD_DIM, acc0, acc1).astype(o_ref.dtype)


def _sb_attention(qkv, batch, seq):
    pairs = SB_HEADS // HEADS_PER_STEP
    nq = seq // SB_BLOCK
    return pl.pallas_call(
        _sb_attn_kernel,
        grid=(batch, pairs, nq),
        in_specs=[
            pl.BlockSpec((1, SB_BLOCK, LANES), lambda b, p, i: (b, i, p)),
            pl.BlockSpec((1, seq, LANES), lambda b, p, i: (b, 0, pairs + p)),
            pl.BlockSpec((1, seq, LANES), lambda b, p, i: (b, 0, 2 * pairs + p)),
        ],
        out_specs=pl.BlockSpec((1, SB_BLOCK, LANES), lambda b, p, i: (b, i, p)),
        out_shape=jax.ShapeDtypeStruct((batch, seq, SB_HEADS * SB_HEAD_DIM), jnp.bfloat16),
        compiler_params=_params(("parallel", "parallel", "arbitrary")),
        name="sb_attn",
    )(qkv, qkv, qkv)


def _mla_proj_kernel(x_ref, pos_ref, freq_ref, g_ref, win_ref, qn_ref, kvn_ref,
                     wuq_ref, wuk_ref, wuv_ref, q_ref, k_ref, v_ref):
    h = _rms(x_ref[...], g_ref[...]).astype(jnp.bfloat16)
    proj = _dot(h, win_ref[...])
    c_q = proj[:, :MLA_Q_LORA]
    c_kv = proj[:, MLA_Q_LORA:MLA_Q_LORA + MLA_KV_LORA]
    kr = proj[:, MLA_Q_LORA + MLA_KV_LORA:MLA_Q_LORA + MLA_KV_LORA + HEAD_SLAB]
    kr_rot = proj[:, MLA_Q_LORA + MLA_KV_LORA + HEAD_SLAB:]

    tm = x_ref.shape[0]
    lane = lax.broadcasted_iota(jnp.int32, (tm, HEAD_SLAB), 1)
    ang = pos_ref[...].astype(jnp.float32) * freq_ref[...]
    is_rope = jnp.logical_and(lane >= MLA_NOPE_DIM, lane < MLA_NOPE_DIM + MLA_ROPE_DIM)
    cos = jnp.where(lane < MLA_NOPE_DIM, 1.0, jnp.where(is_rope, jnp.cos(ang), 0.0))
    sin = jnp.where(is_rope, jnp.sin(ang), 0.0)

    qn = _rms(c_q, qn_ref[...]).astype(jnp.bfloat16)
    q_all = _dot(qn, wuq_ref[...])
    kvn = _rms(c_kv, kvn_ref[...]).astype(jnp.bfloat16)
    k_nope = _dot(kvn, wuk_ref[...])
    v_ref[0] = _dot_nt(wuv_ref[...], kvn).astype(v_ref.dtype)
    k_rope = kr * cos + kr_rot * sin
    width = MLA_HEADS * HEAD_SLAB
    for hd in range(MLA_HEADS):
        sl = slice(hd * HEAD_SLAB, (hd + 1) * HEAD_SLAB)
        rot = slice(width + hd * HEAD_SLAB, width + (hd + 1) * HEAD_SLAB)
        q_ref[:, sl] = (q_all[:, sl] * cos + q_all[:, rot] * sin).astype(q_ref.dtype)
        k_ref[:, sl] = (k_nope[:, sl] + k_rope).astype(k_ref.dtype)


def _mla_weights(w_in, w_uq, w_ukv):
    half = MLA_ROPE_DIM // 2
    pad = HEAD_SLAB - MLA_NOPE_DIM - MLA_ROPE_DIM

    def slab(nope, rope):
        return jnp.concatenate(
            [nope, rope, jnp.zeros(rope.shape[:-1] + (pad,), rope.dtype)], axis=-1)

    def rotate(rope):
        return jnp.concatenate([-rope[..., half:], rope[..., :half]], axis=-1)

    lat = MLA_Q_LORA + MLA_KV_LORA
    kr_w = w_in[:, lat:]
    zeros_in = jnp.zeros((D_MODEL, MLA_NOPE_DIM), w_in.dtype)
    w_in_p = jnp.concatenate(
        [w_in[:, :lat], slab(zeros_in, kr_w), slab(zeros_in, rotate(kr_w))], axis=1)

    uq = w_uq.reshape(MLA_Q_LORA, MLA_HEADS, MLA_NOPE_DIM + MLA_ROPE_DIM)
    uq_nope, uq_rope = uq[..., :MLA_NOPE_DIM], uq[..., MLA_NOPE_DIM:]
    w_uq_p = jnp.concatenate(
        [slab(uq_nope, uq_rope).reshape(MLA_Q_LORA, -1),
         slab(jnp.zeros_like(uq_nope), rotate(uq_rope)).reshape(MLA_Q_LORA, -1)], axis=1)

    ukv = w_ukv.reshape(MLA_KV_LORA, MLA_HEADS, MLA_NOPE_DIM + MLA_V_DIM)
    uk = ukv[..., :MLA_NOPE_DIM]
    w_uk_p = jnp.concatenate([uk, jnp.zeros_like(uk)], axis=-1).reshape(MLA_KV_LORA, -1)
    w_uv = ukv[..., MLA_NOPE_DIM:].reshape(MLA_KV_LORA, -1).T
    bf = jnp.bfloat16
    return w_in_p.astype(bf), w_uq_p.astype(bf), w_uk_p.astype(bf), w_uv.astype(bf)


def _rope_freq_slab():
    inv_freq = ROPE_THETA ** (-jnp.arange(0, MLA_ROPE_DIM, 2, dtype=jnp.float32) / MLA_ROPE_DIM)
    zeros = jnp.zeros((MLA_NOPE_DIM,), jnp.float32)
    pad = jnp.zeros((HEAD_SLAB - MLA_NOPE_DIM - MLA_ROPE_DIM,), jnp.float32)
    return jnp.concatenate([zeros, inv_freq, inv_freq, pad])[None, :]


def _mla_proj(x, pos, g, w_in_p, q_norm, kv_norm, w_uq_p, w_uk_p, w_uv):
    tm = MLA_V_CHUNK
    t = x.shape[0]
    width = MLA_HEADS * HEAD_SLAB
    vwidth = MLA_HEADS * MLA_V_DIM
    row = lambda i: (i, 0)
    return pl.pallas_call(
        _mla_proj_kernel,
        grid=(t // tm,),
        in_specs=[
            pl.BlockSpec((tm, D_MODEL), row),
            pl.BlockSpec((tm, 1), row),
            _const_spec((1, HEAD_SLAB)),
            _const_spec((1, D_MODEL)),
            _const_spec(w_in_p.shape),
            _const_spec((1, MLA_Q_LORA)),
            _const_spec((1, MLA_KV_LORA)),
            _const_spec(w_uq_p.shape),
            _const_spec(w_uk_p.shape),
            _const_spec(w_uv.shape),
        ],
        out_specs=[
            pl.BlockSpec((tm, width), row),
            pl.BlockSpec((tm, width), row),
            pl.BlockSpec((1, vwidth, tm), lambda i: (i, 0, 0)),
        ],
        out_shape=[
            jax.ShapeDtypeStruct((t, width), jnp.bfloat16),
            jax.ShapeDtypeStruct((t, width), jnp.bfloat16),
            jax.ShapeDtypeStruct((t // tm, vwidth, tm), jnp.bfloat16),
        ],
        compiler_params=_params(("parallel",)),
        name="mla_proj",
    )(x, pos, _rope_freq_slab(), g, w_in_p, q_norm, kv_norm, w_uq_p, w_uk_p, w_uv)


def _mla_attn_kernel(q_ref, k_ref, v_ref, o_ref,
                     s0_ref, s1_ref, p0_ref, p1_ref, m_ref, alpha_ref, acc_ref):
    qi = pl.program_id(2)
    tq, tk = MLA_Q_BLOCK, MLA_K_BLOCK
    chunks = tk // MLA_V_CHUNK
    heads = range(HEADS_PER_STEP)
    c = math.log2(math.e) / math.sqrt(MLA_NOPE_DIM + MLA_ROPE_DIM)
    q_heads = [q_ref[0, :, h * HEAD_SLAB:(h + 1) * HEAD_SLAB] for h in heads]
    ones = jnp.ones((MLA_DENOM_ROWS, tk), jnp.bfloat16)
    n_full = (qi * tq) // tk
    s_bufs = (s0_ref, s1_ref)
    p_bufs = (p0_ref, p1_ref)

    def scores(j, s_out):
        start = pl.multiple_of(j * tk, tk)
        for h in heads:
            k_blk = k_ref[0, pl.ds(start, tk), h * HEAD_SLAB:(h + 1) * HEAD_SLAB]
            s_out[h] = _dot_nt(k_blk, q_heads[h])

    def values(j, p_in):
        for h in heads:
            v_t = jnp.concatenate(
                [v_ref[0, j * chunks + ci, h * MLA_V_DIM:(h + 1) * MLA_V_DIM, :]
                 for ci in range(chunks)], axis=1)
            pv = _dot(jnp.concatenate([v_t, ones], axis=0), p_in[h])
            acc_ref[h] = alpha_ref[h] * acc_ref[h] + pv

    def softmax(s_in, p_out, mask):
        for h in heads:
            s = s_in[h] * c
            if mask is not None:
                s = jnp.where(mask, s, -jnp.inf)
            m_old = m_ref[h]
            m_new = jnp.maximum(m_old, jnp.max(s, axis=0, keepdims=True))
            p_out[h] = jnp.exp2(s - m_new).astype(jnp.bfloat16)
            alpha_ref[h] = jnp.exp2(m_old - m_new)
            m_ref[h] = m_new

    def step(i, parity):
        values(jnp.maximum(i - 1, 0), p_bufs[1 - parity])
        scores(i + 1, s_bufs[1 - parity])
        softmax(s_bufs[parity], p_bufs[parity], None)

    def last(parity):
        values(jnp.maximum(n_full - 1, 0), p_bufs[1 - parity])
        key = n_full * tk + lax.broadcasted_iota(jnp.int32, (tk, tq), 0)
        qry = qi * tq + lax.broadcasted_iota(jnp.int32, (tk, tq), 1)
        softmax(s_bufs[parity], p_bufs[parity], key <= qry)
        values(n_full, p_bufs[parity])
        o_t = jnp.concatenate(
            [acc_ref[h, :MLA_V_DIM] / acc_ref[h, MLA_V_DIM:MLA_V_DIM + 1] for h in heads], axis=0)
        o_ref[0] = o_t.T.astype(o_ref.dtype)

    m_ref[...] = jnp.full(m_ref.shape, -jnp.inf, jnp.float32)
    alpha_ref[...] = jnp.ones(alpha_ref.shape, jnp.float32)
    acc_ref[...] = jnp.zeros(acc_ref.shape, jnp.float32)
    p1_ref[...] = jnp.zeros(p1_ref.shape, jnp.bfloat16)
    scores(0, s0_ref)

    def two_steps(ii, carry):
        step(2 * ii, 0)
        step(2 * ii + 1, 1)
        return carry

    lax.fori_loop(0, n_full // 2, two_steps, 0)
    odd = n_full % 2 == 1

    @pl.when(odd)
    def _():
        step(n_full - 1, 0)
        last(1)

    @pl.when(jnp.logical_not(odd))
    def _():
        last(0)


def _mla_attention(q, k, v, batch, seq):
    assert MLA_K_BLOCK % MLA_Q_BLOCK == 0 and seq % MLA_K_BLOCK == 0
    pairs = MLA_HEADS // HEADS_PER_STEP
    nq = seq // MLA_Q_BLOCK
    qk_w = HEADS_PER_STEP * HEAD_SLAB
    return pl.pallas_call(
        _mla_attn_kernel,
        grid=(batch, pairs, nq),
        in_specs=[
            pl.BlockSpec((1, MLA_Q_BLOCK, qk_w), lambda b, p, i: (b, i, p)),
            pl.BlockSpec((1, seq, qk_w), lambda b, p, i: (b, 0, p)),
            pl.BlockSpec((1, seq // MLA_V_CHUNK, HEADS_PER_STEP * MLA_V_DIM, MLA_V_CHUNK),
                         lambda b, p, i: (b, 0, p, 0)),
        ],
        out_specs=pl.BlockSpec((1, MLA_Q_BLOCK, LANES), lambda b, p, i: (b, i, p)),
        out_shape=jax.ShapeDtypeStruct((batch, seq, MLA_HEADS * MLA_V_DIM), jnp.bfloat16),
        scratch_shapes=[
            pltpu.VMEM((HEADS_PER_STEP, MLA_K_BLOCK, MLA_Q_BLOCK), jnp.float32),
            pltpu.VMEM((HEADS_PER_STEP, MLA_K_BLOCK, MLA_Q_BLOCK), jnp.float32),
            pltpu.VMEM((HEADS_PER_STEP, MLA_K_BLOCK, MLA_Q_BLOCK), jnp.bfloat16),
            pltpu.VMEM((HEADS_PER_STEP, MLA_K_BLOCK, MLA_Q_BLOCK), jnp.bfloat16),
            pltpu.VMEM((HEADS_PER_STEP, 1, MLA_Q_BLOCK), jnp.float32),
            pltpu.VMEM((HEADS_PER_STEP, 1, MLA_Q_BLOCK), jnp.float32),
            pltpu.VMEM((HEADS_PER_STEP, MLA_V_DIM + MLA_DENOM_ROWS, MLA_Q_BLOCK), jnp.float32),
        ],
        compiler_params=_params(("parallel", "parallel", "arbitrary")),
        name="mla_attn",
    )(q, k, v)


def kernel(x, positions, norm_g, ffn_w_in, ffn_w_out, sb_w_in, sb_w_out, mla_w_in,
           mla_q_norm, mla_w_uq, mla_kv_norm, mla_w_ukv, mla_w_out):
    batch, seq, _ = x.shape
    t = batch * seq
    bf = jnp.bfloat16
    depth = norm_g.shape[0]
    xt = x.reshape(t, D_MODEL)
    pos = positions.reshape(t, 1)
    for i in range(depth):
        g = norm_g[i][:, None, :]
        xt = _ffn(xt, g[0], g[1], ffn_w_in[i, 0].astype(bf), ffn_w_out[i, 0].astype(bf))
        j = i // 2
        if i % 2 == 0:
            qkv = _sb_qkv(xt, g[2], sb_w_in[j].astype(bf))
            o = _sb_attention(qkv.reshape(batch, seq, -1), batch, seq)
            w_out = sb_w_out[j]
        else:
            w_in_p, w_uq_p, w_uk_p, w_uv = _mla_weights(mla_w_in[j], mla_w_uq[j], mla_w_ukv[j])
            q, k, v = _mla_proj(xt, pos, g[2], w_in_p, mla_q_norm[j][None, :],
                                mla_kv_norm[j][None, :], w_uq_p, w_uk_p, w_uv)
            v_t = v.reshape(batch, seq // MLA_V_CHUNK, MLA_HEADS * MLA_V_DIM, MLA_V_CHUNK)
            o = _mla_attention(q.reshape(batch, seq, -1), k.reshape(batch, seq, -1),
                               v_t, batch, seq)
            w_out = mla_w_out[j]
        xt = _outproj(xt, o.reshape(t, -1), g[3], w_out.astype(bf))
        xt = _ffn(xt, g[4], g[5], ffn_w_in[i, 1].astype(bf), ffn_w_out[i, 1].astype(bf))
    return xt.reshape(batch, seq, D_MODEL)
```

```python
import functools
import math

import jax
import jax.numpy as jnp
from jax import lax
from jax.experimental import pallas as pl
from jax.experimental.pallas import tpu as pltpu

D_MODEL = 1024
EPS = 1e-6
D_FF = 2816
FFN_RESIDUAL_WEIGHT = 0.5

SB_HEADS = 16
SB_HEAD_DIM = 64

MLA_HEADS = 16
MLA_Q_LORA = 256
MLA_KV_LORA = 128
MLA_NOPE_DIM = 64
MLA_ROPE_DIM = 32
MLA_V_DIM = 64
ROPE_THETA = 10000.0

LANES = 128
HEAD_SLAB = 128
HEADS_PER_STEP = 2
VMEM_LIMIT_BYTES = 56 * 1024 * 1024

SB_TAIL_CUTOFF = 104.0
SB_BLOCK = 128
SB_STATIC_BLOCKS = 3
SB_HEADS_PER_STEP = 8

MLA_Q_BLOCK = 512
MLA_K_BLOCK = 512
MLA_V_CHUNK = 256
MLA_DENOM_ROWS = 16


def _rms(x, g):
    return x * lax.rsqrt(jnp.mean(x * x, axis=-1, keepdims=True) + EPS) * g


def _dot(a, b):
    return jnp.dot(a, b, preferred_element_type=jnp.float32)


def _dot_nt(a, b):
    return lax.dot_general(a, b, (((1,), (1,)), ((), ())),
                           preferred_element_type=jnp.float32)


def _params(semantics):
    return pltpu.CompilerParams(dimension_semantics=semantics,
                                vmem_limit_bytes=VMEM_LIMIT_BYTES)


def _const_spec(shape):
    return pl.BlockSpec(shape, lambda *_: (0,) * len(shape),
                        pipeline_mode=pl.Buffered(1))


def _ffn_kernel(x_ref, gpre_ref, gpost_ref, win_ref, wout_ref, o_ref):
    x = x_ref[...]
    xn = _rms(x, gpre_ref[...]).astype(jnp.bfloat16)
    h = _dot(xn, win_ref[...])
    gate = h[:, :D_FF]
    up = h[:, D_FF:]
    act = (gate * jax.nn.sigmoid(gate) * up).astype(jnp.bfloat16)
    f = _dot(act, wout_ref[...])
    o_ref[...] = x + FFN_RESIDUAL_WEIGHT * _rms(f, gpost_ref[...])


def _ffn(x, g_pre, g_post, w_in, w_out, tm=256):
    t = x.shape[0]
    return pl.pallas_call(
        _ffn_kernel,
        grid=(t // tm,),
        in_specs=[
            pl.BlockSpec((tm, D_MODEL), lambda i: (i, 0)),
            _const_spec((1, D_MODEL)),
            _const_spec((1, D_MODEL)),
            _const_spec((D_MODEL, 2 * D_FF)),
            _const_spec((D_FF, D_MODEL)),
        ],
        out_specs=pl.BlockSpec((tm, D_MODEL), lambda i: (i, 0)),
        out_shape=jax.ShapeDtypeStruct((t, D_MODEL), jnp.float32),
        compiler_params=_params(("parallel",)),
        name="ffn",
    )(x, g_pre, g_post, w_in, w_out)


def _outproj_kernel(x_ref, o_ref, g_ref, w_ref, y_ref):
    m = _dot(o_ref[...], w_ref[...])
    y_ref[...] = x_ref[...] + _rms(m, g_ref[...])


def _outproj(x, o, g, w, tm=512):
    t = x.shape[0]
    return pl.pallas_call(
        _outproj_kernel,
        grid=(t // tm,),
        in_specs=[
            pl.BlockSpec((tm, D_MODEL), lambda i: (i, 0)),
            pl.BlockSpec((tm, D_MODEL), lambda i: (i, 0)),
            _const_spec((1, D_MODEL)),
            _const_spec((D_MODEL, D_MODEL)),
        ],
        out_specs=pl.BlockSpec((tm, D_MODEL), lambda i: (i, 0)),
        out_shape=jax.ShapeDtypeStruct((t, D_MODEL), jnp.float32),
        compiler_params=_params(("parallel",)),
        name="outproj",
    )(x, o, g, w)


def _sb_qkv_kernel(x_ref, g_ref, wqk_ref, wvt_ref, qk_ref, vt_ref):
    h = _rms(x_ref[...], g_ref[...]).astype(jnp.bfloat16)
    width = SB_HEADS * SB_HEAD_DIM
    y = _dot(h, wqk_ref[...])
    qk_ref[:, :width] = (y[:, :width] * (1.0 / math.sqrt(SB_HEAD_DIM))).astype(qk_ref.dtype)
    qk_ref[:, width:] = y[:, width:].astype(qk_ref.dtype)
    v_t = _dot_nt(wvt_ref[...], h).astype(vt_ref.dtype)
    for ci in range(vt_ref.shape[0]):
        vt_ref[ci] = v_t[:, ci * SB_BLOCK:(ci + 1) * SB_BLOCK]


def _sb_qkv(x, g, w_qk, w_vt, tm=512):
    t = x.shape[0]
    width = SB_HEADS * SB_HEAD_DIM
    chunks = tm // SB_BLOCK
    return pl.pallas_call(
        _sb_qkv_kernel,
        grid=(t // tm,),
        in_specs=[
            pl.BlockSpec((tm, D_MODEL), lambda i: (i, 0)),
            _const_spec((1, D_MODEL)),
            _const_spec((D_MODEL, 2 * width)),
            _const_spec((width, D_MODEL)),
        ],
        out_specs=[
            pl.BlockSpec((tm, 2 * width), lambda i: (i, 0)),
            pl.BlockSpec((chunks, width, SB_BLOCK), lambda i: (i, 0, 0)),
        ],
        out_shape=[
            jax.ShapeDtypeStruct((t, 2 * width), jnp.bfloat16),
            jax.ShapeDtypeStruct((t // SB_BLOCK, width, SB_BLOCK), jnp.bfloat16),
        ],
        compiler_params=_params(("parallel",)),
        name="sb_qkv",
    )(x, g, w_qk, w_vt)


def _sb_attn_kernel(q_ref, k_ref, v_ref, o_ref):
    qi = pl.program_id(2)
    blk = SB_BLOCK
    heads = range(SB_HEADS_PER_STEP)
    lane = lax.broadcasted_iota(jnp.int32, (blk, LANES), 1)
    key = lax.broadcasted_iota(jnp.int32, (blk, blk), 0)
    qry = lax.broadcasted_iota(jnp.int32, (blk, blk), 1)
    causal = key < qry
    ur = lax.broadcasted_iota(jnp.int32, (blk, 2 * blk), 0)
    uc = lax.broadcasted_iota(jnp.int32, (blk, 2 * blk), 1)
    cum = jnp.where((uc % blk) > ur, 1.0, 0.0).astype(jnp.bfloat16)
    slabs = [slice((h // 2) * LANES, (h // 2 + 1) * LANES) for h in heads]
    q_heads = []
    for h in heads:
        q = q_ref[0, :, slabs[h]]
        q_heads.append(jnp.where((lane // SB_HEAD_DIM) == h % 2, q, jnp.zeros_like(q)))

    def logits(j, h):
        start = pl.multiple_of(j * blk, blk)
        return _dot_nt(k_ref[0, pl.ds(start, blk), slabs[h]], q_heads[h])

    def softplus_parts(z, mask):
        sp = jnp.maximum(z, 0.0) + jnp.log(1.0 + jnp.exp2(jnp.abs(z) * (-math.log2(math.e))))
        sp_m = sp if mask is None else jnp.where(mask, sp, 0.0)
        hi = sp_m.astype(jnp.bfloat16)
        lo = (sp_m - hi.astype(jnp.float32)).astype(jnp.bfloat16)
        return sp, jnp.concatenate([hi, lo], axis=0), jnp.sum(sp_m, axis=0, keepdims=True)

    def weights(z, sp, inner, tail, mask):
        w = jnp.exp(z - sp - inner - tail)
        return (w if mask is None else jnp.where(mask, w, 0.0)).astype(jnp.bfloat16)

    def values(j, h, a, valid=None):
        v_t = v_ref[0, j, h * SB_HEAD_DIM:(h + 1) * SB_HEAD_DIM, :]
        if valid is not None:
            v_t = jnp.where(valid, v_t, jnp.zeros_like(v_t))
        return _dot(v_t, a)

    tiles = [(d, h) for d in range(SB_STATIC_BLOCKS) for h in heads]
    js = [jnp.maximum(qi - d, 0) for d in range(SB_STATIC_BLOCKS)]
    masks = [causal] + [None] * (SB_STATIC_BLOCKS - 1)
    valid = [None] + [qi >= d for d in range(1, SB_STATIC_BLOCKS)]
    z = {(d, h): logits(js[d], h) for d, h in tiles}
    parts = {t: softplus_parts(z[t], masks[t[0]]) for t in tiles}
    inner = {t: _dot(cum, parts[t][1]) for t in tiles}
    tails = [jnp.zeros((1, blk), jnp.float32) for _ in heads]
    a = {}
    for d, h in tiles:
        a[d, h] = weights(z[d, h], parts[d, h][0], inner[d, h], tails[h], masks[d])
        total = parts[d, h][2]
        tails[h] = tails[h] + (total if d == 0 else jnp.where(valid[d], total, 0.0))
    accs = [sum(values(js[d], h, a[d, h], valid[d]) for d in range(SB_STATIC_BLOCKS))
            for h in heads]

    def live(state):
        smallest = functools.reduce(jnp.minimum, [state[1 + 2 * h] for h in heads])
        return jnp.logical_and(state[0] >= 0, jnp.min(smallest) < SB_TAIL_CUTOFF)

    def older(state):
        j = state[0]
        out = [j - 1]
        for h in heads:
            tail, acc = state[1 + 2 * h], state[2 + 2 * h]
            zz = logits(j, h)
            sp, hilo, total = softplus_parts(zz, None)
            acc = acc + values(j, h, weights(zz, sp, _dot(cum, hilo), tail, None))
            out += [tail + total, acc]
        return tuple(out)

    init = [qi - SB_STATIC_BLOCKS]
    for h in heads:
        init += [tails[h], accs[h]]
    state = lax.while_loop(live, older, tuple(init))
    o_t = jnp.concatenate([state[2 + 2 * h] for h in heads], axis=0)
    o_ref[0] = o_t.T.astype(o_ref.dtype)


def _sb_attention(qk, v_t, batch, seq):
    groups = SB_HEADS // SB_HEADS_PER_STEP
    width = SB_HEADS_PER_STEP * SB_HEAD_DIM
    nq = seq // SB_BLOCK
    return pl.pallas_call(
        _sb_attn_kernel,
        grid=(batch, groups, nq),
        in_specs=[
            pl.BlockSpec((1, SB_BLOCK, width), lambda b, p, i: (b, i, p)),
            pl.BlockSpec((1, seq, width), lambda b, p, i: (b, 0, groups + p)),
            pl.BlockSpec((1, nq, width, SB_BLOCK), lambda b, p, i: (b, 0, p, 0)),
        ],
        out_specs=pl.BlockSpec((1, SB_BLOCK, width), lambda b, p, i: (b, i, p)),
        out_shape=jax.ShapeDtypeStruct((batch, seq, SB_HEADS * SB_HEAD_DIM), jnp.bfloat16),
        compiler_params=_params(("parallel", "parallel", "arbitrary")),
        name="sb_attn",
    )(qk, qk, v_t)


def _mla_proj_kernel(x_ref, pos_ref, freq_ref, g_ref, win_ref, qn_ref, kvn_ref,
                     wuq_ref, wuk_ref, wuv_ref, q_ref, k_ref, v_ref):
    h = _rms(x_ref[...], g_ref[...]).astype(jnp.bfloat16)
    proj = _dot(h, win_ref[...])
    c_q = proj[:, :MLA_Q_LORA]
    c_kv = proj[:, MLA_Q_LORA:MLA_Q_LORA + MLA_KV_LORA]
    kr = proj[:, MLA_Q_LORA + MLA_KV_LORA:MLA_Q_LORA + MLA_KV_LORA + HEAD_SLAB]
    kr_rot = proj[:, MLA_Q_LORA + MLA_KV_LORA + HEAD_SLAB:]

    tm = x_ref.shape[0]
    lane = lax.broadcasted_iota(jnp.int32, (tm, HEAD_SLAB), 1)
    ang = pos_ref[...].astype(jnp.float32) * freq_ref[...]
    is_rope = jnp.logical_and(lane >= MLA_NOPE_DIM, lane < MLA_NOPE_DIM + MLA_ROPE_DIM)
    cos = jnp.where(lane < MLA_NOPE_DIM, 1.0, jnp.where(is_rope, jnp.cos(ang), 0.0))
    sin = jnp.where(is_rope, jnp.sin(ang), 0.0)

    qn = _rms(c_q, qn_ref[...]).astype(jnp.bfloat16)
    q_all = _dot(qn, wuq_ref[...])
    kvn = _rms(c_kv, kvn_ref[...]).astype(jnp.bfloat16)
    k_nope = _dot(kvn, wuk_ref[...])
    v_ref[0] = _dot_nt(wuv_ref[...], kvn).astype(v_ref.dtype)
    k_rope = kr * cos + kr_rot * sin
    width = MLA_HEADS * HEAD_SLAB
    for hd in range(MLA_HEADS):
        sl = slice(hd * HEAD_SLAB, (hd + 1) * HEAD_SLAB)
        rot = slice(width + hd * HEAD_SLAB, width + (hd + 1) * HEAD_SLAB)
        q_ref[:, sl] = (q_all[:, sl] * cos + q_all[:, rot] * sin).astype(q_ref.dtype)
        k_ref[:, sl] = (k_nope[:, sl] + k_rope).astype(k_ref.dtype)


def _mla_weights(w_in, w_uq, w_ukv):
    half = MLA_ROPE_DIM // 2
    pad = HEAD_SLAB - MLA_NOPE_DIM - MLA_ROPE_DIM

    def slab(nope, rope):
        return jnp.concatenate(
            [nope, rope, jnp.zeros(rope.shape[:-1] + (pad,), rope.dtype)], axis=-1)

    def rotate(rope):
        return jnp.concatenate([-rope[..., half:], rope[..., :half]], axis=-1)

    lat = MLA_Q_LORA + MLA_KV_LORA
    kr_w = w_in[:, lat:]
    zeros_in = jnp.zeros((D_MODEL, MLA_NOPE_DIM), w_in.dtype)
    w_in_p = jnp.concatenate(
        [w_in[:, :lat], slab(zeros_in, kr_w), slab(zeros_in, rotate(kr_w))], axis=1)

    uq = w_uq.reshape(MLA_Q_LORA, MLA_HEADS, MLA_NOPE_DIM + MLA_ROPE_DIM)
    uq_nope, uq_rope = uq[..., :MLA_NOPE_DIM], uq[..., MLA_NOPE_DIM:]
    w_uq_p = jnp.concatenate(
        [slab(uq_nope, uq_rope).reshape(MLA_Q_LORA, -1),
         slab(jnp.zeros_like(uq_nope), rotate(uq_rope)).reshape(MLA_Q_LORA, -1)], axis=1)

    ukv = w_ukv.reshape(MLA_KV_LORA, MLA_HEADS, MLA_NOPE_DIM + MLA_V_DIM)
    uk = ukv[..., :MLA_NOPE_DIM]
    w_uk_p = jnp.concatenate([uk, jnp.zeros_like(uk)], axis=-1).reshape(MLA_KV_LORA, -1)
    w_uv = ukv[..., MLA_NOPE_DIM:].reshape(MLA_KV_LORA, -1).T
    bf = jnp.bfloat16
    return w_in_p.astype(bf), w_uq_p.astype(bf), w_uk_p.astype(bf), w_uv.astype(bf)


def _rope_freq_slab():
    inv_freq = ROPE_THETA ** (-jnp.arange(0, MLA_ROPE_DIM, 2, dtype=jnp.float32) / MLA_ROPE_DIM)
    zeros = jnp.zeros((MLA_NOPE_DIM,), jnp.float32)
    pad = jnp.zeros((HEAD_SLAB - MLA_NOPE_DIM - MLA_ROPE_DIM,), jnp.float32)
    return jnp.concatenate([zeros, inv_freq, inv_freq, pad])[None, :]


def _mla_proj(x, pos, g, w_in_p, q_norm, kv_norm, w_uq_p, w_uk_p, w_uv):
    tm = MLA_V_CHUNK
    t = x.shape[0]
    width = MLA_HEADS * HEAD_SLAB
    vwidth = MLA_HEADS * MLA_V_DIM
    row = lambda i: (i, 0)
    return pl.pallas_call(
        _mla_proj_kernel,
        grid=(t // tm,),
        in_specs=[
            pl.BlockSpec((tm, D_MODEL), row),
            pl.BlockSpec((tm, 1), row),
            _const_spec((1, HEAD_SLAB)),
            _const_spec((1, D_MODEL)),
            _const_spec(w_in_p.shape),
            _const_spec((1, MLA_Q_LORA)),
            _const_spec((1, MLA_KV_LORA)),
            _const_spec(w_uq_p.shape),
            _const_spec(w_uk_p.shape),
            _const_spec(w_uv.shape),
        ],
        out_specs=[
            pl.BlockSpec((tm, width), row),
            pl.BlockSpec((tm, width), row),
            pl.BlockSpec((1, vwidth, tm), lambda i: (i, 0, 0)),
        ],
        out_shape=[
            jax.ShapeDtypeStruct((t, width), jnp.bfloat16),
            jax.ShapeDtypeStruct((t, width), jnp.bfloat16),
            jax.ShapeDtypeStruct((t // tm, vwidth, tm), jnp.bfloat16),
        ],
        compiler_params=_params(("parallel",)),
        name="mla_proj",
    )(x, pos, _rope_freq_slab(), g, w_in_p, q_norm, kv_norm, w_uq_p, w_uk_p, w_uv)


def _mla_attn_kernel(q_ref, k_ref, v_ref, o_ref,
                     s0_ref, s1_ref, p0_ref, p1_ref, m_ref, alpha_ref, acc_ref):
    qi = pl.program_id(2)
    tq, tk = MLA_Q_BLOCK, MLA_K_BLOCK
    chunks = tk // MLA_V_CHUNK
    heads = range(HEADS_PER_STEP)
    c = math.log2(math.e) / math.sqrt(MLA_NOPE_DIM + MLA_ROPE_DIM)
    q_heads = [q_ref[0, :, h * HEAD_SLAB:(h + 1) * HEAD_SLAB] for h in heads]
    ones = jnp.ones((MLA_DENOM_ROWS, tk), jnp.bfloat16)
    n_full = (qi * tq) // tk
    s_bufs = (s0_ref, s1_ref)
    p_bufs = (p0_ref, p1_ref)

    def scores(j, s_out):
        start = pl.multiple_of(j * tk, tk)
        for h in heads:
            k_blk = k_ref[0, pl.ds(start, tk), h * HEAD_SLAB:(h + 1) * HEAD_SLAB]
            s_out[h] = _dot_nt(k_blk, q_heads[h])

    def values(j, p_in):
        for h in heads:
            v_t = jnp.concatenate(
                [v_ref[0, j * chunks + ci, h * MLA_V_DIM:(h + 1) * MLA_V_DIM, :]
                 for ci in range(chunks)], axis=1)
            pv = _dot(jnp.concatenate([v_t, ones], axis=0), p_in[h])
            acc_ref[h] = alpha_ref[h] * acc_ref[h] + pv

    def softmax(s_in, p_out, mask):
        for h in heads:
            s = s_in[h] * c
            if mask is not None:
                s = jnp.where(mask, s, -jnp.inf)
            m_old = m_ref[h]
            m_new = jnp.maximum(m_old, jnp.max(s, axis=0, keepdims=True))
            p_out[h] = jnp.exp2(s - m_new).astype(jnp.bfloat16)
            alpha_ref[h] = jnp.exp2(m_old - m_new)
            m_ref[h] = m_new

    def step(i, parity):
        values(jnp.maximum(i - 1, 0), p_bufs[1 - parity])
        scores(i + 1, s_bufs[1 - parity])
        softmax(s_bufs[parity], p_bufs[parity], None)

    def last(parity):
        values(jnp.maximum(n_full - 1, 0), p_bufs[1 - parity])
        key = n_full * tk + lax.broadcasted_iota(jnp.int32, (tk, tq), 0)
        qry = qi * tq + lax.broadcasted_iota(jnp.int32, (tk, tq), 1)
        softmax(s_bufs[parity], p_bufs[parity], key <= qry)
        values(n_full, p_bufs[parity])
        o_t = jnp.concatenate(
            [acc_ref[h, :MLA_V_DIM] / acc_ref[h, MLA_V_DIM:MLA_V_DIM + 1] for h in heads], axis=0)
        o_ref[0] = o_t.T.astype(o_ref.dtype)

    m_ref[...] = jnp.full(m_ref.shape, -jnp.inf, jnp.float32)
    alpha_ref[...] = jnp.ones(alpha_ref.shape, jnp.float32)
    acc_ref[...] = jnp.zeros(acc_ref.shape, jnp.float32)
    p1_ref[...] = jnp.zeros(p1_ref.shape, jnp.bfloat16)
    scores(0, s0_ref)

    def two_steps(ii, carry):
        step(2 * ii, 0)
        step(2 * ii + 1, 1)
        return carry

    lax.fori_loop(0, n_full // 2, two_steps, 0)
    odd = n_full % 2 == 1

    @pl.when(odd)
    def _():
        step(n_full - 1, 0)
        last(1)

    @pl.when(jnp.logical_not(odd))
    def _():
        last(0)


def _mla_attention(q, k, v, batch, seq):
    assert MLA_K_BLOCK % MLA_Q_BLOCK == 0 and seq % MLA_K_BLOCK == 0
    pairs = MLA_HEADS // HEADS_PER_STEP
    nq = seq // MLA_Q_BLOCK
    qk_w = HEADS_PER_STEP * HEAD_SLAB
    return pl.pallas_call(
        _mla_attn_kernel,
        grid=(batch, pairs, nq),
        in_specs=[
            pl.BlockSpec((1, MLA_Q_BLOCK, qk_w), lambda b, p, i: (b, i, p)),
            pl.BlockSpec((1, seq, qk_w), lambda b, p, i: (b, 0, p)),
            pl.BlockSpec((1, seq // MLA_V_CHUNK, HEADS_PER_STEP * MLA_V_DIM, MLA_V_CHUNK),
                         lambda b, p, i: (b, 0, p, 0)),
        ],
        out_specs=pl.BlockSpec((1, MLA_Q_BLOCK, LANES), lambda b, p, i: (b, i, p)),
        out_shape=jax.ShapeDtypeStruct((batch, seq, MLA_HEADS * MLA_V_DIM), jnp.bfloat16),
        scratch_shapes=[
            pltpu.VMEM((HEADS_PER_STEP, MLA_K_BLOCK, MLA_Q_BLOCK), jnp.float32),
            pltpu.VMEM((HEADS_PER_STEP, MLA_K_BLOCK, MLA_Q_BLOCK), jnp.float32),
            pltpu.VMEM((HEADS_PER_STEP, MLA_K_BLOCK, MLA_Q_BLOCK), jnp.bfloat16),
            pltpu.VMEM((HEADS_PER_STEP, MLA_K_BLOCK, MLA_Q_BLOCK), jnp.bfloat16),
            pltpu.VMEM((HEADS_PER_STEP, 1, MLA_Q_BLOCK), jnp.float32),
            pltpu.VMEM((HEADS_PER_STEP, 1, MLA_Q_BLOCK), jnp.float32),
            pltpu.VMEM((HEADS_PER_STEP, MLA_V_DIM + MLA_DENOM_ROWS, MLA_Q_BLOCK), jnp.float32),
        ],
        compiler_params=_params(("parallel", "parallel", "arbitrary")),
        name="mla_attn",
    )(q, k, v)


def kernel(x, positions, norm_g, ffn_w_in, ffn_w_out, sb_w_in, sb_w_out, mla_w_in,
           mla_q_norm, mla_w_uq, mla_kv_norm, mla_w_ukv, mla_w_out):
    batch, seq, _ = x.shape
    t = batch * seq
    bf = jnp.bfloat16
    depth = norm_g.shape[0]
    xt = x.reshape(t, D_MODEL)
    pos = positions.reshape(t, 1)
    for i in range(depth):
        g = norm_g[i][:, None, :]
        xt = _ffn(xt, g[0], g[1], ffn_w_in[i, 0].astype(bf), ffn_w_out[i, 0].astype(bf))
        j = i // 2
        if i % 2 == 0:
            width = SB_HEADS * SB_HEAD_DIM
            w_qk = sb_w_in[j][:, :2 * width].astype(bf)
            w_vt = sb_w_in[j][:, 2 * width:].T.astype(bf)
            qk, v_t = _sb_qkv(xt, g[2], w_qk, w_vt)
            o = _sb_attention(qk.reshape(batch, seq, -1),
                              v_t.reshape(batch, seq // SB_BLOCK, width, SB_BLOCK), batch, seq)
            w_out = sb_w_out[j]
        else:
            w_in_p, w_uq_p, w_uk_p, w_uv = _mla_weights(mla_w_in[j], mla_w_uq[j], mla_w_ukv[j])
            q, k, v = _mla_proj(xt, pos, g[2], w_in_p, mla_q_norm[j][None, :],
                                mla_kv_norm[j][None, :], w_uq_p, w_uk_p, w_uv)
            v_t = v.reshape(batch, seq // MLA_V_CHUNK, MLA_HEADS * MLA_V_DIM, MLA_V_CHUNK)
            o = _mla_attention(q.reshape(batch, seq, -1), k.reshape(batch, seq, -1),
                               v_t, batch, seq)
            w_out = mla_w_out[j]
        xt = _outproj(xt, o.reshape(t, -1), g[3], w_out.astype(bf))
        xt = _ffn(xt, g[4], g[5], ffn_w_in[i, 1].astype(bf), ffn_w_out[i, 1].astype(bf))
    return xt.reshape(batch, seq, D_MODEL)
```

```python
import functools
import math

import jax
import jax.numpy as jnp
from jax import lax
from jax.experimental import pallas as pl
from jax.experimental.pallas import tpu as pltpu

D_MODEL = 1024
EPS = 1e-6
D_FF = 2816
FFN_RESIDUAL_WEIGHT = 0.5

SB_HEADS = 16
SB_HEAD_DIM = 64

MLA_HEADS = 16
MLA_Q_LORA = 256
MLA_KV_LORA = 128
MLA_NOPE_DIM = 64
MLA_ROPE_DIM = 32
MLA_V_DIM = 64
ROPE_THETA = 10000.0

LANES = 128
HEAD_SLAB = 128
HEADS_PER_STEP = 2
VMEM_LIMIT_BYTES = 56 * 1024 * 1024

SB_TAIL_CUTOFF = 104.0
SB_BLOCK = 128
SB_STATIC_BLOCKS = 3
SB_HEADS_PER_STEP = 8

MLA_Q_BLOCK = 512
MLA_K_BLOCK = 512
MLA_V_CHUNK = 256
MLA_SCORE_SCALE = math.log2(math.e) / math.sqrt(MLA_NOPE_DIM + MLA_ROPE_DIM)
MLA_DENOM_ROWS = 16


def _rms(x, g):
    return x * lax.rsqrt(jnp.mean(x * x, axis=-1, keepdims=True) + EPS) * g


def _dot(a, b):
    return jnp.dot(a, b, preferred_element_type=jnp.float32)


def _dot_nt(a, b):
    return lax.dot_general(a, b, (((1,), (1,)), ((), ())),
                           preferred_element_type=jnp.float32)


def _params(semantics):
    return pltpu.CompilerParams(dimension_semantics=semantics,
                                vmem_limit_bytes=VMEM_LIMIT_BYTES)


def _const_spec(shape):
    return pl.BlockSpec(shape, lambda *_: (0,) * len(shape),
                        pipeline_mode=pl.Buffered(1))


FFN_TOKENS = 512


def _ffn_block(x, gpre_ref, gpost_ref, win_ref, wout_ref):
    xn = _rms(x, gpre_ref[...]).astype(jnp.bfloat16)
    h = _dot(xn, win_ref[...])
    gate = h[:, :D_FF]
    up = h[:, D_FF:]
    act = (gate * jax.nn.sigmoid(gate) * up).astype(jnp.bfloat16)
    f = _dot(act, wout_ref[...])
    return x + FFN_RESIDUAL_WEIGHT * _rms(f, gpost_ref[...])


def _ffn_kernel(x_ref, gpre_ref, gpost_ref, win_ref, wout_ref, y_ref):
    y_ref[...] = _ffn_block(x_ref[...], gpre_ref, gpost_ref, win_ref, wout_ref)


def _mixer_out_ffn_kernel(x_ref, o_ref, wo_ref, gmix_ref, gpre_ref, gpost_ref, win_ref, wout_ref,
                          y_ref):
    x = x_ref[...] + _rms(_dot(o_ref[...], wo_ref[...]), gmix_ref[...])
    y_ref[...] = _ffn_block(x, gpre_ref, gpost_ref, win_ref, wout_ref)


def _ffn_specs():
    return [
        _const_spec((1, D_MODEL)),
        _const_spec((1, D_MODEL)),
        _const_spec((D_MODEL, 2 * D_FF)),
        _const_spec((D_FF, D_MODEL)),
    ]


def _ffn(x, g_pre, g_post, w_in, w_out):
    t = x.shape[0]
    rows = pl.BlockSpec((FFN_TOKENS, D_MODEL), lambda i: (i, 0))
    return pl.pallas_call(
        _ffn_kernel,
        grid=(t // FFN_TOKENS,),
        in_specs=[rows] + _ffn_specs(),
        out_specs=rows,
        out_shape=jax.ShapeDtypeStruct((t, D_MODEL), jnp.float32),
        compiler_params=_params(("parallel",)),
        name="ffn",
    )(x, g_pre, g_post, w_in, w_out)


def _mixer_out_ffn(x, o, w_o, g_mix, g_pre, g_post, w_in, w_out):
    t = x.shape[0]
    rows = pl.BlockSpec((FFN_TOKENS, D_MODEL), lambda i: (i, 0))
    return pl.pallas_call(
        _mixer_out_ffn_kernel,
        grid=(t // FFN_TOKENS,),
        in_specs=[rows, rows, _const_spec((D_MODEL, D_MODEL)), _const_spec((1, D_MODEL))]
        + _ffn_specs(),
        out_specs=rows,
        out_shape=jax.ShapeDtypeStruct((t, D_MODEL), jnp.float32),
        compiler_params=_params(("parallel",)),
        name="mixer_out_ffn",
    )(x, o, w_o, g_mix, g_pre, g_post, w_in, w_out)


def _sb_qkv_kernel(x_ref, g_ref, wqk_ref, wvt_ref, qk_ref, vt_ref):
    h = _rms(x_ref[...], g_ref[...]).astype(jnp.bfloat16)
    width = SB_HEADS * SB_HEAD_DIM
    y = _dot(h, wqk_ref[...])
    qk_ref[:, :width] = (y[:, :width] * (1.0 / math.sqrt(SB_HEAD_DIM))).astype(qk_ref.dtype)
    qk_ref[:, width:] = y[:, width:].astype(qk_ref.dtype)
    v_t = _dot_nt(wvt_ref[...], h).astype(vt_ref.dtype)
    for ci in range(vt_ref.shape[0]):
        vt_ref[ci] = v_t[:, ci * SB_BLOCK:(ci + 1) * SB_BLOCK]


def _sb_qkv(x, g, w_qk, w_vt, tm=512):
    t = x.shape[0]
    width = SB_HEADS * SB_HEAD_DIM
    chunks = tm // SB_BLOCK
    return pl.pallas_call(
        _sb_qkv_kernel,
        grid=(t // tm,),
        in_specs=[
            pl.BlockSpec((tm, D_MODEL), lambda i: (i, 0)),
            _const_spec((1, D_MODEL)),
            _const_spec((D_MODEL, 2 * width)),
            _const_spec((width, D_MODEL)),
        ],
        out_specs=[
            pl.BlockSpec((tm, 2 * width), lambda i: (i, 0)),
            pl.BlockSpec((chunks, width, SB_BLOCK), lambda i: (i, 0, 0)),
        ],
        out_shape=[
            jax.ShapeDtypeStruct((t, 2 * width), jnp.bfloat16),
            jax.ShapeDtypeStruct((t // SB_BLOCK, width, SB_BLOCK), jnp.bfloat16),
        ],
        compiler_params=_params(("parallel",)),
        name="sb_qkv",
    )(x, g, w_qk, w_vt)


def _sb_attn_kernel(q_ref, k_ref, v_ref, o_ref):
    qi = pl.program_id(2)
    blk = SB_BLOCK
    heads = range(SB_HEADS_PER_STEP)
    lane = lax.broadcasted_iota(jnp.int32, (blk, LANES), 1)
    key = lax.broadcasted_iota(jnp.int32, (blk, blk), 0)
    qry = lax.broadcasted_iota(jnp.int32, (blk, blk), 1)
    causal = key < qry
    ur = lax.broadcasted_iota(jnp.int32, (blk, 2 * blk), 0)
    uc = lax.broadcasted_iota(jnp.int32, (blk, 2 * blk), 1)
    cum = jnp.where((uc % blk) > ur, 1.0, 0.0).astype(jnp.bfloat16)
    slabs = [slice((h // 2) * LANES, (h // 2 + 1) * LANES) for h in heads]
    q_heads = []
    for h in heads:
        q = q_ref[0, :, slabs[h]]
        q_heads.append(jnp.where((lane // SB_HEAD_DIM) == h % 2, q, jnp.zeros_like(q)))

    def logits(j, h):
        start = pl.multiple_of(j * blk, blk)
        return _dot_nt(k_ref[0, pl.ds(start, blk), slabs[h]], q_heads[h])

    def softplus_parts(z, mask):
        sp = jnp.maximum(z, 0.0) + jnp.log(1.0 + jnp.exp2(jnp.abs(z) * (-math.log2(math.e))))
        sp_m = sp if mask is None else jnp.where(mask, sp, 0.0)
        hi = sp_m.astype(jnp.bfloat16)
        lo = (sp_m - hi.astype(jnp.float32)).astype(jnp.bfloat16)
        return sp, jnp.concatenate([hi, lo], axis=0), jnp.sum(sp_m, axis=0, keepdims=True)

    def weights(z, sp, inner, tail, mask):
        w = jnp.exp(z - sp - inner - tail)
        return (w if mask is None else jnp.where(mask, w, 0.0)).astype(jnp.bfloat16)

    def values(j, h, a, valid=None):
        v_t = v_ref[0, j, h * SB_HEAD_DIM:(h + 1) * SB_HEAD_DIM, :]
        if valid is not None:
            v_t = jnp.where(valid, v_t, jnp.zeros_like(v_t))
        return _dot(v_t, a)

    tiles = [(d, h) for d in range(SB_STATIC_BLOCKS) for h in heads]
    js = [jnp.maximum(qi - d, 0) for d in range(SB_STATIC_BLOCKS)]
    masks = [causal] + [None] * (SB_STATIC_BLOCKS - 1)
    valid = [None] + [qi >= d for d in range(1, SB_STATIC_BLOCKS)]
    z = {(d, h): logits(js[d], h) for d, h in tiles}
    parts = {t: softplus_parts(z[t], masks[t[0]]) for t in tiles}
    inner = {t: _dot(cum, parts[t][1]) for t in tiles}
    tails = [jnp.zeros((1, blk), jnp.float32) for _ in heads]
    a = {}
    for d, h in tiles:
        a[d, h] = weights(z[d, h], parts[d, h][0], inner[d, h], tails[h], masks[d])
        total = parts[d, h][2]
        tails[h] = tails[h] + (total if d == 0 else jnp.where(valid[d], total, 0.0))
    accs = [sum(values(js[d], h, a[d, h], valid[d]) for d in range(SB_STATIC_BLOCKS))
            for h in heads]

    def live(state):
        smallest = functools.reduce(jnp.minimum, [state[1 + 2 * h] for h in heads])
        return jnp.logical_and(state[0] >= 0, jnp.min(smallest) < SB_TAIL_CUTOFF)

    def older(state):
        j = state[0]
        out = [j - 1]
        for h in heads:
            tail, acc = state[1 + 2 * h], state[2 + 2 * h]
            zz = logits(j, h)
            sp, hilo, total = softplus_parts(zz, None)
            acc = acc + values(j, h, weights(zz, sp, _dot(cum, hilo), tail, None))
            out += [tail + total, acc]
        return tuple(out)

    init = [qi - SB_STATIC_BLOCKS]
    for h in heads:
        init += [tails[h], accs[h]]
    state = lax.while_loop(live, older, tuple(init))
    o_t = jnp.concatenate([state[2 + 2 * h] for h in heads], axis=0)
    o_ref[0] = o_t.T.astype(o_ref.dtype)


def _sb_attention(qk, v_t, batch, seq):
    groups = SB_HEADS // SB_HEADS_PER_STEP
    width = SB_HEADS_PER_STEP * SB_HEAD_DIM
    nq = seq // SB_BLOCK
    return pl.pallas_call(
        _sb_attn_kernel,
        grid=(batch, groups, nq),
        in_specs=[
            pl.BlockSpec((1, SB_BLOCK, width), lambda b, p, i: (b, i, p)),
            pl.BlockSpec((1, seq, width), lambda b, p, i: (b, 0, groups + p)),
            pl.BlockSpec((1, nq, width, SB_BLOCK), lambda b, p, i: (b, 0, p, 0)),
        ],
        out_specs=pl.BlockSpec((1, SB_BLOCK, width), lambda b, p, i: (b, i, p)),
        out_shape=jax.ShapeDtypeStruct((batch, seq, SB_HEADS * SB_HEAD_DIM), jnp.bfloat16),
        compiler_params=_params(("parallel", "parallel", "arbitrary")),
        name="sb_attn",
    )(qk, qk, v_t)


def _mla_proj_kernel(x_ref, pos_ref, freq_ref, g_ref, win_ref, qn_ref, kvn_ref,
                     wuq_ref, wuk_ref, wuv_ref, q_ref, k_ref, v_ref):
    h = _rms(x_ref[...], g_ref[...]).astype(jnp.bfloat16)
    proj = _dot(h, win_ref[...])
    c_q = proj[:, :MLA_Q_LORA]
    c_kv = proj[:, MLA_Q_LORA:MLA_Q_LORA + MLA_KV_LORA]
    kr = proj[:, MLA_Q_LORA + MLA_KV_LORA:MLA_Q_LORA + MLA_KV_LORA + HEAD_SLAB]
    kr_rot = proj[:, MLA_Q_LORA + MLA_KV_LORA + HEAD_SLAB:]

    tm = x_ref.shape[0]
    lane = lax.broadcasted_iota(jnp.int32, (tm, HEAD_SLAB), 1)
    ang = pos_ref[...].astype(jnp.float32) * freq_ref[...]
    is_rope = jnp.logical_and(lane >= MLA_NOPE_DIM, lane < MLA_NOPE_DIM + MLA_ROPE_DIM)
    cos = jnp.where(lane < MLA_NOPE_DIM, 1.0, jnp.where(is_rope, jnp.cos(ang), 0.0))
    sin = jnp.where(is_rope, jnp.sin(ang), 0.0)

    qn = _rms(c_q, qn_ref[...]).astype(jnp.bfloat16)
    q_all = _dot(qn, wuq_ref[...])
    kvn = _rms(c_kv, kvn_ref[...]).astype(jnp.bfloat16)
    k_nope = _dot(kvn, wuk_ref[...])
    v_ref[0] = _dot_nt(wuv_ref[...], kvn).astype(v_ref.dtype)
    k_rope = kr * cos + kr_rot * sin
    cos_q = cos * MLA_SCORE_SCALE
    sin_q = sin * MLA_SCORE_SCALE
    width = MLA_HEADS * HEAD_SLAB
    for hd in range(MLA_HEADS):
        sl = slice(hd * HEAD_SLAB, (hd + 1) * HEAD_SLAB)
        rot = slice(width + hd * HEAD_SLAB, width + (hd + 1) * HEAD_SLAB)
        q_ref[:, sl] = (q_all[:, sl] * cos_q + q_all[:, rot] * sin_q).astype(q_ref.dtype)
        k_ref[:, sl] = (k_nope[:, sl] + k_rope).astype(k_ref.dtype)


def _mla_weights(w_in, w_uq, w_ukv):
    half = MLA_ROPE_DIM // 2
    pad = HEAD_SLAB - MLA_NOPE_DIM - MLA_ROPE_DIM

    def slab(nope, rope):
        return jnp.concatenate(
            [nope, rope, jnp.zeros(rope.shape[:-1] + (pad,), rope.dtype)], axis=-1)

    def rotate(rope):
        return jnp.concatenate([-rope[..., half:], rope[..., :half]], axis=-1)

    lat = MLA_Q_LORA + MLA_KV_LORA
    kr_w = w_in[:, lat:]
    zeros_in = jnp.zeros((D_MODEL, MLA_NOPE_DIM), w_in.dtype)
    w_in_p = jnp.concatenate(
        [w_in[:, :lat], slab(zeros_in, kr_w), slab(zeros_in, rotate(kr_w))], axis=1)

    uq = w_uq.reshape(MLA_Q_LORA, MLA_HEADS, MLA_NOPE_DIM + MLA_ROPE_DIM)
    uq_nope, uq_rope = uq[..., :MLA_NOPE_DIM], uq[..., MLA_NOPE_DIM:]
    w_uq_p = jnp.concatenate(
        [slab(uq_nope, uq_rope).reshape(MLA_Q_LORA, -1),
         slab(jnp.zeros_like(uq_nope), rotate(uq_rope)).reshape(MLA_Q_LORA, -1)], axis=1)

    ukv = w_ukv.reshape(MLA_KV_LORA, MLA_HEADS, MLA_NOPE_DIM + MLA_V_DIM)
    uk = ukv[..., :MLA_NOPE_DIM]
    w_uk_p = jnp.concatenate([uk, jnp.zeros_like(uk)], axis=-1).reshape(MLA_KV_LORA, -1)
    w_uv = ukv[..., MLA_NOPE_DIM:].reshape(MLA_KV_LORA, -1).T
    bf = jnp.bfloat16
    return w_in_p.astype(bf), w_uq_p.astype(bf), w_uk_p.astype(bf), w_uv.astype(bf)


def _rope_freq_slab():
    inv_freq = ROPE_THETA ** (-jnp.arange(0, MLA_ROPE_DIM, 2, dtype=jnp.float32) / MLA_ROPE_DIM)
    zeros = jnp.zeros((MLA_NOPE_DIM,), jnp.float32)
    pad = jnp.zeros((HEAD_SLAB - MLA_NOPE_DIM - MLA_ROPE_DIM,), jnp.float32)
    return jnp.concatenate([zeros, inv_freq, inv_freq, pad])[None, :]


def _mla_proj(x, pos, g, w_in_p, q_norm, kv_norm, w_uq_p, w_uk_p, w_uv):
    tm = MLA_V_CHUNK
    t = x.shape[0]
    width = MLA_HEADS * HEAD_SLAB
    vwidth = MLA_HEADS * MLA_V_DIM
    row = lambda i: (i, 0)
    return pl.pallas_call(
        _mla_proj_kernel,
        grid=(t // tm,),
        in_specs=[
            pl.BlockSpec((tm, D_MODEL), row),
            pl.BlockSpec((tm, 1), row),
            _const_spec((1, HEAD_SLAB)),
            _const_spec((1, D_MODEL)),
            _const_spec(w_in_p.shape),
            _const_spec((1, MLA_Q_LORA)),
            _const_spec((1, MLA_KV_LORA)),
            _const_spec(w_uq_p.shape),
            _const_spec(w_uk_p.shape),
            _const_spec(w_uv.shape),
        ],
        out_specs=[
            pl.BlockSpec((tm, width), row),
            pl.BlockSpec((tm, width), row),
            pl.BlockSpec((1, vwidth, tm), lambda i: (i, 0, 0)),
        ],
        out_shape=[
            jax.ShapeDtypeStruct((t, width), jnp.bfloat16),
            jax.ShapeDtypeStruct((t, width), jnp.bfloat16),
            jax.ShapeDtypeStruct((t // tm, vwidth, tm), jnp.bfloat16),
        ],
        compiler_params=_params(("parallel",)),
        name="mla_proj",
    )(x, pos, _rope_freq_slab(), g, w_in_p, q_norm, kv_norm, w_uq_p, w_uk_p, w_uv)


def _mla_attn_kernel(q_ref, k_ref, v_ref, o_ref,
                     s0_ref, s1_ref, p0_ref, p1_ref, mx_ref, m_ref, alpha_ref, acc_ref):
    qi = pl.program_id(2)
    tq, tk = MLA_Q_BLOCK, MLA_K_BLOCK
    chunks = tk // MLA_V_CHUNK
    heads = range(HEADS_PER_STEP)
    q_heads = [q_ref[0, :, h * HEAD_SLAB:(h + 1) * HEAD_SLAB] for h in heads]
    ones = jnp.ones((MLA_DENOM_ROWS, tk), jnp.bfloat16)
    n_full = (qi * tq) // tk
    s_bufs = (s0_ref, s1_ref)
    p_bufs = (p0_ref, p1_ref)

    def scores(j, parity):
        start = pl.multiple_of(j * tk, tk)
        for h in heads:
            k_blk = k_ref[0, pl.ds(start, tk), h * HEAD_SLAB:(h + 1) * HEAD_SLAB]
            s = _dot_nt(k_blk, q_heads[h])
            s_bufs[parity][h] = s
            mx_ref[parity, h] = jnp.max(s, axis=0, keepdims=True)

    def values(j, parity):
        for h in heads:
            v_t = jnp.concatenate(
                [v_ref[0, j * chunks + ci, h * MLA_V_DIM:(h + 1) * MLA_V_DIM, :]
                 for ci in range(chunks)], axis=1)
            pv = _dot(jnp.concatenate([v_t, ones], axis=0), p_bufs[parity][h])
            acc_ref[h] = alpha_ref[parity, h] * acc_ref[h] + pv

    def softmax(parity, mask):
        for h in heads:
            s = s_bufs[parity][h]
            if mask is None:
                block_max = mx_ref[parity, h]
            else:
                s = jnp.where(mask, s, -jnp.inf)
                block_max = jnp.max(s, axis=0, keepdims=True)
            m_old = m_ref[h]
            m_new = jnp.maximum(m_old, block_max)
            p_bufs[parity][h] = jnp.exp2(s - m_new).astype(jnp.bfloat16)
            alpha_ref[parity, h] = jnp.exp2(m_old - m_new)
            m_ref[h] = m_new

    def step(i, parity):
        softmax(parity, None)
        scores(i + 1, 1 - parity)
        values(jnp.maximum(i - 1, 0), 1 - parity)

    def last(parity):
        values(jnp.maximum(n_full - 1, 0), 1 - parity)
        key = n_full * tk + lax.broadcasted_iota(jnp.int32, (tk, tq), 0)
        qry = qi * tq + lax.broadcasted_iota(jnp.int32, (tk, tq), 1)
        softmax(parity, key <= qry)
        values(n_full, parity)
        o_t = jnp.concatenate(
            [acc_ref[h, :MLA_V_DIM] / acc_ref[h, MLA_V_DIM:MLA_V_DIM + 1] for h in heads], axis=0)
        o_ref[0] = o_t.T.astype(o_ref.dtype)

    m_ref[...] = jnp.full(m_ref.shape, -jnp.inf, jnp.float32)
    alpha_ref[...] = jnp.ones(alpha_ref.shape, jnp.float32)
    acc_ref[...] = jnp.zeros(acc_ref.shape, jnp.float32)
    p1_ref[...] = jnp.zeros(p1_ref.shape, jnp.bfloat16)
    scores(0, 0)

    def two_steps(ii, carry):
        step(2 * ii, 0)
        step(2 * ii + 1, 1)
        return carry

    lax.fori_loop(0, n_full // 2, two_steps, 0)
    odd = n_full % 2 == 1

    @pl.when(odd)
    def _():
        step(n_full - 1, 0)
        last(1)

    @pl.when(jnp.logical_not(odd))
    def _():
        last(0)


def _mla_attention(q, k, v, batch, seq):
    assert MLA_K_BLOCK % MLA_Q_BLOCK == 0 and seq % MLA_K_BLOCK == 0
    pairs = MLA_HEADS // HEADS_PER_STEP
    nq = seq // MLA_Q_BLOCK
    qk_w = HEADS_PER_STEP * HEAD_SLAB
    return pl.pallas_call(
        _mla_attn_kernel,
        grid=(batch, pairs, nq),
        in_specs=[
            pl.BlockSpec((1, MLA_Q_BLOCK, qk_w), lambda b, p, i: (b, i, p)),
            pl.BlockSpec((1, seq, qk_w), lambda b, p, i: (b, 0, p)),
            pl.BlockSpec((1, seq // MLA_V_CHUNK, HEADS_PER_STEP * MLA_V_DIM, MLA_V_CHUNK),
                         lambda b, p, i: (b, 0, p, 0)),
        ],
        out_specs=pl.BlockSpec((1, MLA_Q_BLOCK, LANES), lambda b, p, i: (b, i, p)),
        out_shape=jax.ShapeDtypeStruct((batch, seq, MLA_HEADS * MLA_V_DIM), jnp.bfloat16),
        scratch_shapes=[
            pltpu.VMEM((HEADS_PER_STEP, MLA_K_BLOCK, MLA_Q_BLOCK), jnp.float32),
            pltpu.VMEM((HEADS_PER_STEP, MLA_K_BLOCK, MLA_Q_BLOCK), jnp.float32),
            pltpu.VMEM((HEADS_PER_STEP, MLA_K_BLOCK, MLA_Q_BLOCK), jnp.bfloat16),
            pltpu.VMEM((HEADS_PER_STEP, MLA_K_BLOCK, MLA_Q_BLOCK), jnp.bfloat16),
            pltpu.VMEM((2, HEADS_PER_STEP, 1, MLA_Q_BLOCK), jnp.float32),
            pltpu.VMEM((HEADS_PER_STEP, 1, MLA_Q_BLOCK), jnp.float32),
            pltpu.VMEM((2, HEADS_PER_STEP, 1, MLA_Q_BLOCK), jnp.float32),
            pltpu.VMEM((HEADS_PER_STEP, MLA_V_DIM + MLA_DENOM_ROWS, MLA_Q_BLOCK), jnp.float32),
        ],
        compiler_params=_params(("parallel", "parallel", "arbitrary")),
        name="mla_attn",
    )(q, k, v)


def kernel(x, positions, norm_g, ffn_w_in, ffn_w_out, sb_w_in, sb_w_out, mla_w_in,
           mla_q_norm, mla_w_uq, mla_kv_norm, mla_w_ukv, mla_w_out):
    batch, seq, _ = x.shape
    t = batch * seq
    bf = jnp.bfloat16
    depth = norm_g.shape[0]
    xt = x.reshape(t, D_MODEL)
    pos = positions.reshape(t, 1)
    for i in range(depth):
        g = norm_g[i][:, None, :]
        xt = _ffn(xt, g[0], g[1], ffn_w_in[i, 0].astype(bf), ffn_w_out[i, 0].astype(bf))
        j = i // 2
        if i % 2 == 0:
            width = SB_HEADS * SB_HEAD_DIM
            w_qk = sb_w_in[j][:, :2 * width].astype(bf)
            w_vt = sb_w_in[j][:, 2 * width:].T.astype(bf)
            qk, v_t = _sb_qkv(xt, g[2], w_qk, w_vt)
            o = _sb_attention(qk.reshape(batch, seq, -1),
                              v_t.reshape(batch, seq // SB_BLOCK, width, SB_BLOCK), batch, seq)
            w_out = sb_w_out[j]
        else:
            w_in_p, w_uq_p, w_uk_p, w_uv = _mla_weights(mla_w_in[j], mla_w_uq[j], mla_w_ukv[j])
            q, k, v = _mla_proj(xt, pos, g[2], w_in_p, mla_q_norm[j][None, :],
                                mla_kv_norm[j][None, :], w_uq_p, w_uk_p, w_uv)
            v_t = v.reshape(batch, seq // MLA_V_CHUNK, MLA_HEADS * MLA_V_DIM, MLA_V_CHUNK)
            o = _mla_attention(q.reshape(batch, seq, -1), k.reshape(batch, seq, -1),
                               v_t, batch, seq)
            w_out = mla_w_out[j]
        xt = _mixer_out_ffn(xt, o.reshape(t, -1), w_out.astype(bf), g[3], g[4], g[5],
                            ffn_w_in[i, 1].astype(bf), ffn_w_out[i, 1].astype(bf))
    return xt.reshape(batch, seq, D_MODEL)
```

```python
import functools
import math

import jax
import jax.numpy as jnp
from jax import lax
from jax.experimental import pallas as pl
from jax.experimental.pallas import tpu as pltpu

D_MODEL = 1024
EPS = 1e-6
D_FF = 2816
FFN_RESIDUAL_WEIGHT = 0.5

SB_HEADS = 16
SB_HEAD_DIM = 64

MLA_HEADS = 16
MLA_Q_LORA = 256
MLA_KV_LORA = 128
MLA_NOPE_DIM = 64
MLA_ROPE_DIM = 32
MLA_V_DIM = 64
ROPE_THETA = 10000.0

LANES = 128
HEAD_SLAB = 128
HEADS_PER_STEP = 2
VMEM_LIMIT_BYTES = 56 * 1024 * 1024

SB_TAIL_CUTOFF = 104.0
SB_BLOCK = 128
SB_STATIC_BLOCKS = 3
SB_HEADS_PER_STEP = 8

MLA_Q_BLOCK = 1024
MLA_K_BLOCK = 512
MLA_UNROLL = 2
MLA_V_CHUNK = 256
MLA_SCORE_SCALE = math.log2(math.e) / math.sqrt(MLA_NOPE_DIM + MLA_ROPE_DIM)
MLA_DENOM_ROWS = 16


def _rms(x, g):
    return x * lax.rsqrt(jnp.mean(x * x, axis=-1, keepdims=True) + EPS) * g


def _dot(a, b):
    return jnp.dot(a, b, preferred_element_type=jnp.float32)


def _dot_nt(a, b):
    return lax.dot_general(a, b, (((1,), (1,)), ((), ())),
                           preferred_element_type=jnp.float32)


def _params(semantics):
    return pltpu.CompilerParams(dimension_semantics=semantics,
                                vmem_limit_bytes=VMEM_LIMIT_BYTES)


def _const_spec(shape):
    return pl.BlockSpec(shape, lambda *_: (0,) * len(shape),
                        pipeline_mode=pl.Buffered(1))


FFN_TOKENS = 512


def _ffn_block(x, gpre_ref, gpost_ref, win_ref, wout_ref):
    xn = _rms(x, gpre_ref[...]).astype(jnp.bfloat16)
    h = _dot(xn, win_ref[...])
    gate = h[:, :D_FF]
    up = h[:, D_FF:]
    act = (gate * jax.nn.sigmoid(gate) * up).astype(jnp.bfloat16)
    f = _dot(act, wout_ref[...])
    return x + FFN_RESIDUAL_WEIGHT * _rms(f, gpost_ref[...])


def _ffn_kernel(x_ref, gpre_ref, gpost_ref, win_ref, wout_ref, y_ref):
    y_ref[...] = _ffn_block(x_ref[...], gpre_ref, gpost_ref, win_ref, wout_ref)


def _mixer_out_ffn_kernel(x_ref, o_ref, wo_ref, gmix_ref, gpre_ref, gpost_ref, win_ref, wout_ref,
                          y_ref):
    x = x_ref[...] + _rms(_dot(o_ref[...], wo_ref[...]), gmix_ref[...])
    y_ref[...] = _ffn_block(x, gpre_ref, gpost_ref, win_ref, wout_ref)


def _ffn_specs(layer, half):
    def weight(rows, cols):
        return pl.BlockSpec((None, None, rows, cols), lambda *_: (layer, half, 0, 0),
                            pipeline_mode=pl.Buffered(1))
    return [
        _const_spec((1, D_MODEL)),
        _const_spec((1, D_MODEL)),
        weight(D_MODEL, 2 * D_FF),
        weight(D_FF, D_MODEL),
    ]


def _ffn(x, g_pre, g_post, w_in, w_out, layer, half):
    t = x.shape[0]
    rows = pl.BlockSpec((FFN_TOKENS, D_MODEL), lambda i: (i, 0))
    return pl.pallas_call(
        _ffn_kernel,
        grid=(t // FFN_TOKENS,),
        in_specs=[rows] + _ffn_specs(layer, half),
        out_specs=rows,
        out_shape=jax.ShapeDtypeStruct((t, D_MODEL), jnp.float32),
        compiler_params=_params(("parallel",)),
        name="ffn",
    )(x, g_pre, g_post, w_in, w_out)


def _mixer_out_ffn(x, o, w_o, g_mix, g_pre, g_post, w_in, w_out, layer, half):
    t = x.shape[0]
    rows = pl.BlockSpec((FFN_TOKENS, D_MODEL), lambda i: (i, 0))
    return pl.pallas_call(
        _mixer_out_ffn_kernel,
        grid=(t // FFN_TOKENS,),
        in_specs=[rows, rows, _const_spec((D_MODEL, D_MODEL)), _const_spec((1, D_MODEL))]
        + _ffn_specs(layer, half),
        out_specs=rows,
        out_shape=jax.ShapeDtypeStruct((t, D_MODEL), jnp.float32),
        compiler_params=_params(("parallel",)),
        name="mixer_out_ffn",
    )(x, o, w_o, g_mix, g_pre, g_post, w_in, w_out)


def _sb_qkv_kernel(x_ref, g_ref, wqk_ref, wvt_ref, qk_ref, vt_ref):
    h = _rms(x_ref[...], g_ref[...]).astype(jnp.bfloat16)
    width = SB_HEADS * SB_HEAD_DIM
    y = _dot(h, wqk_ref[...])
    qk_ref[:, :width] = (y[:, :width] * (1.0 / math.sqrt(SB_HEAD_DIM))).astype(qk_ref.dtype)
    qk_ref[:, width:] = y[:, width:].astype(qk_ref.dtype)
    v_t = _dot_nt(wvt_ref[...], h).astype(vt_ref.dtype)
    for ci in range(vt_ref.shape[0]):
        vt_ref[ci] = v_t[:, ci * SB_BLOCK:(ci + 1) * SB_BLOCK]


def _sb_qkv(x, g, w_qk, w_vt, tm=512):
    t = x.shape[0]
    width = SB_HEADS * SB_HEAD_DIM
    chunks = tm // SB_BLOCK
    return pl.pallas_call(
        _sb_qkv_kernel,
        grid=(t // tm,),
        in_specs=[
            pl.BlockSpec((tm, D_MODEL), lambda i: (i, 0)),
            _const_spec((1, D_MODEL)),
            _const_spec((D_MODEL, 2 * width)),
            _const_spec((width, D_MODEL)),
        ],
        out_specs=[
            pl.BlockSpec((tm, 2 * width), lambda i: (i, 0)),
            pl.BlockSpec((chunks, width, SB_BLOCK), lambda i: (i, 0, 0)),
        ],
        out_shape=[
            jax.ShapeDtypeStruct((t, 2 * width), jnp.bfloat16),
            jax.ShapeDtypeStruct((t // SB_BLOCK, width, SB_BLOCK), jnp.bfloat16),
        ],
        compiler_params=_params(("parallel",)),
        name="sb_qkv",
    )(x, g, w_qk, w_vt)


def _sb_attn_kernel(q_ref, k_ref, v_ref, o_ref):
    qi = pl.program_id(2)
    blk = SB_BLOCK
    heads = range(SB_HEADS_PER_STEP)
    lane = lax.broadcasted_iota(jnp.int32, (blk, LANES), 1)
    key = lax.broadcasted_iota(jnp.int32, (blk, blk), 0)
    qry = lax.broadcasted_iota(jnp.int32, (blk, blk), 1)
    causal = key < qry
    ur = lax.broadcasted_iota(jnp.int32, (blk, 2 * blk), 0)
    uc = lax.broadcasted_iota(jnp.int32, (blk, 2 * blk), 1)
    cum = jnp.where((uc % blk) > ur, 1.0, 0.0).astype(jnp.bfloat16)
    slabs = [slice((h // 2) * LANES, (h // 2 + 1) * LANES) for h in heads]
    q_heads = []
    for h in heads:
        q = q_ref[0, :, slabs[h]]
        q_heads.append(jnp.where((lane // SB_HEAD_DIM) == h % 2, q, jnp.zeros_like(q)))

    def logits(j, h):
        start = pl.multiple_of(j * blk, blk)
        return _dot_nt(k_ref[0, pl.ds(start, blk), slabs[h]], q_heads[h])

    def softplus_parts(z, mask):
        sp = jnp.maximum(z, 0.0) + jnp.log(1.0 + jnp.exp2(jnp.abs(z) * (-math.log2(math.e))))
        sp_m = sp if mask is None else jnp.where(mask, sp, 0.0)
        hi = sp_m.astype(jnp.bfloat16)
        lo = (sp_m - hi.astype(jnp.float32)).astype(jnp.bfloat16)
        return sp, jnp.concatenate([hi, lo], axis=0), jnp.sum(sp_m, axis=0, keepdims=True)

    def weights(z, sp, inner, tail, mask):
        w = jnp.exp(z - sp - inner - tail)
        return (w if mask is None else jnp.where(mask, w, 0.0)).astype(jnp.bfloat16)

    def values(j, h, a, valid=None):
        v_t = v_ref[0, j, h * SB_HEAD_DIM:(h + 1) * SB_HEAD_DIM, :]
        if valid is not None:
            v_t = jnp.where(valid, v_t, jnp.zeros_like(v_t))
        return _dot(v_t, a)

    tiles = [(d, h) for d in range(SB_STATIC_BLOCKS) for h in heads]
    js = [jnp.maximum(qi - d, 0) for d in range(SB_STATIC_BLOCKS)]
    masks = [causal] + [None] * (SB_STATIC_BLOCKS - 1)
    valid = [None] + [qi >= d for d in range(1, SB_STATIC_BLOCKS)]
    z = {(d, h): logits(js[d], h) for d, h in tiles}
    parts = {t: softplus_parts(z[t], masks[t[0]]) for t in tiles}
    inner = {t: _dot(cum, parts[t][1]) for t in tiles}
    tails = [jnp.zeros((1, blk), jnp.float32) for _ in heads]
    a = {}
    for d, h in tiles:
        a[d, h] = weights(z[d, h], parts[d, h][0], inner[d, h], tails[h], masks[d])
        total = parts[d, h][2]
        tails[h] = tails[h] + (total if d == 0 else jnp.where(valid[d], total, 0.0))
    accs = [sum(values(js[d], h, a[d, h], valid[d]) for d in range(SB_STATIC_BLOCKS))
            for h in heads]

    def live(state):
        smallest = functools.reduce(jnp.minimum, [state[1 + 2 * h] for h in heads])
        return jnp.logical_and(state[0] >= 0, jnp.min(smallest) < SB_TAIL_CUTOFF)

    def older(state):
        j = state[0]
        out = [j - 1]
        for h in heads:
            tail, acc = state[1 + 2 * h], state[2 + 2 * h]
            zz = logits(j, h)
            sp, hilo, total = softplus_parts(zz, None)
            acc = acc + values(j, h, weights(zz, sp, _dot(cum, hilo), tail, None))
            out += [tail + total, acc]
        return tuple(out)

    init = [qi - SB_STATIC_BLOCKS]
    for h in heads:
        init += [tails[h], accs[h]]
    state = lax.while_loop(live, older, tuple(init))
    o_t = jnp.concatenate([state[2 + 2 * h] for h in heads], axis=0)
    o_ref[0] = o_t.T.astype(o_ref.dtype)


def _sb_attention(qk, v_t, batch, seq):
    groups = SB_HEADS // SB_HEADS_PER_STEP
    width = SB_HEADS_PER_STEP * SB_HEAD_DIM
    nq = seq // SB_BLOCK
    return pl.pallas_call(
        _sb_attn_kernel,
        grid=(batch, groups, nq),
        in_specs=[
            pl.BlockSpec((1, SB_BLOCK, width), lambda b, p, i: (b, i, p)),
            pl.BlockSpec((1, seq, width), lambda b, p, i: (b, 0, groups + p)),
            pl.BlockSpec((1, nq, width, SB_BLOCK), lambda b, p, i: (b, 0, p, 0)),
        ],
        out_specs=pl.BlockSpec((1, SB_BLOCK, width), lambda b, p, i: (b, i, p)),
        out_shape=jax.ShapeDtypeStruct((batch, seq, SB_HEADS * SB_HEAD_DIM), jnp.bfloat16),
        compiler_params=_params(("parallel", "parallel", "arbitrary")),
        name="sb_attn",
    )(qk, qk, v_t)


def _mla_proj_kernel(x_ref, pos_ref, freq_ref, g_ref, win_ref, qn_ref, kvn_ref,
                     wuq_ref, wuk_ref, wuv_ref, q_ref, k_ref, v_ref):
    h = _rms(x_ref[...], g_ref[...]).astype(jnp.bfloat16)
    proj = _dot(h, win_ref[...])
    c_q = proj[:, :MLA_Q_LORA]
    c_kv = proj[:, MLA_Q_LORA:MLA_Q_LORA + MLA_KV_LORA]
    kr = proj[:, MLA_Q_LORA + MLA_KV_LORA:MLA_Q_LORA + MLA_KV_LORA + HEAD_SLAB]
    kr_rot = proj[:, MLA_Q_LORA + MLA_KV_LORA + HEAD_SLAB:]

    tm = x_ref.shape[0]
    lane = lax.broadcasted_iota(jnp.int32, (tm, HEAD_SLAB), 1)
    ang = pos_ref[...].astype(jnp.float32) * freq_ref[...]
    is_rope = jnp.logical_and(lane >= MLA_NOPE_DIM, lane < MLA_NOPE_DIM + MLA_ROPE_DIM)
    cos = jnp.where(lane < MLA_NOPE_DIM, 1.0, jnp.where(is_rope, jnp.cos(ang), 0.0))
    sin = jnp.where(is_rope, jnp.sin(ang), 0.0)

    qn = _rms(c_q, qn_ref[...]).astype(jnp.bfloat16)
    q_all = _dot(qn, wuq_ref[...])
    kvn = _rms(c_kv, kvn_ref[...]).astype(jnp.bfloat16)
    k_nope = _dot(kvn, wuk_ref[...])
    v_ref[0] = _dot_nt(wuv_ref[...], kvn).astype(v_ref.dtype)
    k_rope = kr * cos + kr_rot * sin
    cos_q = cos * MLA_SCORE_SCALE
    sin_q = sin * MLA_SCORE_SCALE
    width = MLA_HEADS * HEAD_SLAB
    for hd in range(MLA_HEADS):
        sl = slice(hd * HEAD_SLAB, (hd + 1) * HEAD_SLAB)
        rot = slice(width + hd * HEAD_SLAB, width + (hd + 1) * HEAD_SLAB)
        q_ref[:, sl] = (q_all[:, sl] * cos_q + q_all[:, rot] * sin_q).astype(q_ref.dtype)
        k_ref[:, sl] = (k_nope[:, sl] + k_rope).astype(k_ref.dtype)


def _mla_weights(w_in, w_uq, w_ukv):
    half = MLA_ROPE_DIM // 2
    pad = HEAD_SLAB - MLA_NOPE_DIM - MLA_ROPE_DIM

    def slab(nope, rope):
        return jnp.concatenate(
            [nope, rope, jnp.zeros(rope.shape[:-1] + (pad,), rope.dtype)], axis=-1)

    def rotate(rope):
        return jnp.concatenate([-rope[..., half:], rope[..., :half]], axis=-1)

    lat = MLA_Q_LORA + MLA_KV_LORA
    kr_w = w_in[:, lat:]
    zeros_in = jnp.zeros((D_MODEL, MLA_NOPE_DIM), w_in.dtype)
    w_in_p = jnp.concatenate(
        [w_in[:, :lat], slab(zeros_in, kr_w), slab(zeros_in, rotate(kr_w))], axis=1)

    uq = w_uq.reshape(MLA_Q_LORA, MLA_HEADS, MLA_NOPE_DIM + MLA_ROPE_DIM)
    uq_nope, uq_rope = uq[..., :MLA_NOPE_DIM], uq[..., MLA_NOPE_DIM:]
    w_uq_p = jnp.concatenate(
        [slab(uq_nope, uq_rope).reshape(MLA_Q_LORA, -1),
         slab(jnp.zeros_like(uq_nope), rotate(uq_rope)).reshape(MLA_Q_LORA, -1)], axis=1)

    ukv = w_ukv.reshape(MLA_KV_LORA, MLA_HEADS, MLA_NOPE_DIM + MLA_V_DIM)
    uk = ukv[..., :MLA_NOPE_DIM]
    w_uk_p = jnp.concatenate([uk, jnp.zeros_like(uk)], axis=-1).reshape(MLA_KV_LORA, -1)
    w_uv = ukv[..., MLA_NOPE_DIM:].reshape(MLA_KV_LORA, -1).T
    bf = jnp.bfloat16
    return w_in_p.astype(bf), w_uq_p.astype(bf), w_uk_p.astype(bf), w_uv.astype(bf)


def _rope_freq_slab():
    inv_freq = ROPE_THETA ** (-jnp.arange(0, MLA_ROPE_DIM, 2, dtype=jnp.float32) / MLA_ROPE_DIM)
    zeros = jnp.zeros((MLA_NOPE_DIM,), jnp.float32)
    pad = jnp.zeros((HEAD_SLAB - MLA_NOPE_DIM - MLA_ROPE_DIM,), jnp.float32)
    return jnp.concatenate([zeros, inv_freq, inv_freq, pad])[None, :]


def _mla_proj(x, pos, g, w_in_p, q_norm, kv_norm, w_uq_p, w_uk_p, w_uv):
    tm = MLA_V_CHUNK
    t = x.shape[0]
    width = MLA_HEADS * HEAD_SLAB
    vwidth = MLA_HEADS * MLA_V_DIM
    row = lambda i: (i, 0)
    return pl.pallas_call(
        _mla_proj_kernel,
        grid=(t // tm,),
        in_specs=[
            pl.BlockSpec((tm, D_MODEL), row),
            pl.BlockSpec((tm, 1), row),
            _const_spec((1, HEAD_SLAB)),
            _const_spec((1, D_MODEL)),
            _const_spec(w_in_p.shape),
            _const_spec((1, MLA_Q_LORA)),
            _const_spec((1, MLA_KV_LORA)),
            _const_spec(w_uq_p.shape),
            _const_spec(w_uk_p.shape),
            _const_spec(w_uv.shape),
        ],
        out_specs=[
            pl.BlockSpec((tm, width), row),
            pl.BlockSpec((tm, width), row),
            pl.BlockSpec((1, vwidth, tm), lambda i: (i, 0, 0)),
        ],
        out_shape=[
            jax.ShapeDtypeStruct((t, width), jnp.bfloat16),
            jax.ShapeDtypeStruct((t, width), jnp.bfloat16),
            jax.ShapeDtypeStruct((t // tm, vwidth, tm), jnp.bfloat16),
        ],
        compiler_params=_params(("parallel",)),
        name="mla_proj",
    )(x, pos, _rope_freq_slab(), g, w_in_p, q_norm, kv_norm, w_uq_p, w_uk_p, w_uv)


def _mla_attn_kernel(q_ref, k_ref, v_ref, o_ref,
                     s0_ref, s1_ref, p0_ref, p1_ref, mx_ref, m_ref, alpha_ref, acc_ref):
    qi = pl.program_id(2)
    tq, tk = MLA_Q_BLOCK, MLA_K_BLOCK
    chunks = tk // MLA_V_CHUNK
    heads = range(HEADS_PER_STEP)
    q_heads = [q_ref[0, :, h * HEAD_SLAB:(h + 1) * HEAD_SLAB] for h in heads]
    ones = jnp.ones((MLA_DENOM_ROWS, tk), jnp.bfloat16)
    ratio = tq // tk
    n_full = qi * ratio
    s_bufs = (s0_ref, s1_ref)
    p_bufs = (p0_ref, p1_ref)

    def scores(j, parity, first=0):
        start = pl.multiple_of(j * tk, tk)
        for h in heads:
            k_blk = k_ref[0, pl.ds(start, tk), h * HEAD_SLAB:(h + 1) * HEAD_SLAB]
            s = _dot_nt(k_blk, q_heads[h][first:])
            s_bufs[parity][h, :, first:] = s
            mx_ref[parity, h, :, first:] = jnp.max(s, axis=0, keepdims=True)

    def values(j, parity, first=0):
        for h in heads:
            v_t = jnp.concatenate(
                [v_ref[0, j * chunks + ci, h * MLA_V_DIM:(h + 1) * MLA_V_DIM, :]
                 for ci in range(chunks)], axis=1)
            pv = _dot(jnp.concatenate([v_t, ones], axis=0), p_bufs[parity][h, :, first:])
            acc_ref[h, :, first:] = alpha_ref[parity, h, :, first:] * acc_ref[h, :, first:] + pv

    def softmax(parity, mask, first=0):
        for h in heads:
            s = s_bufs[parity][h, :, first:]
            if mask is None:
                block_max = mx_ref[parity, h, :, first:]
            else:
                s = jnp.where(mask, s, -jnp.inf)
                block_max = jnp.max(s, axis=0, keepdims=True)
            m_old = m_ref[h, :, first:]
            m_new = jnp.maximum(m_old, block_max)
            p_bufs[parity][h, :, first:] = jnp.exp2(s - m_new).astype(jnp.bfloat16)
            alpha_ref[parity, h, :, first:] = jnp.exp2(m_old - m_new)
            m_ref[h, :, first:] = m_new

    def step(i, parity):
        softmax(parity, None)
        scores(i + 1, 1 - parity)
        values(jnp.maximum(i - 1, 0), 1 - parity)

    def diagonal_steps():
        for off in range(ratio):
            if off + 1 < ratio:
                scores(n_full + off + 1, (off + 1) % 2, (off + 1) * tk)
            key = lax.broadcasted_iota(jnp.int32, (tk, tq - off * tk), 0)
            qry = lax.broadcasted_iota(jnp.int32, (tk, tq - off * tk), 1)
            softmax(off % 2, key <= qry, off * tk)
            values(jnp.maximum(n_full + off - 1, 0), (off + 1) % 2, max(off - 1, 0) * tk)
        values(n_full + ratio - 1, (ratio - 1) % 2, (ratio - 1) * tk)
        o_t = jnp.concatenate(
            [acc_ref[h, :MLA_V_DIM] / acc_ref[h, MLA_V_DIM:MLA_V_DIM + 1] for h in heads], axis=0)
        o_ref[0] = o_t.T.astype(o_ref.dtype)

    m_ref[...] = jnp.full(m_ref.shape, -jnp.inf, jnp.float32)
    alpha_ref[...] = jnp.ones(alpha_ref.shape, jnp.float32)
    acc_ref[...] = jnp.zeros(acc_ref.shape, jnp.float32)
    p1_ref[...] = jnp.zeros(p1_ref.shape, jnp.bfloat16)
    scores(0, 0)

    def unrolled_steps(ii, carry):
        for u in range(MLA_UNROLL):
            step(MLA_UNROLL * ii + u, u % 2)
        return carry

    lax.fori_loop(0, n_full // MLA_UNROLL, unrolled_steps, 0)
    done = (n_full // MLA_UNROLL) * MLA_UNROLL
    span = MLA_UNROLL // 2
    while span >= 2:
        @pl.when((n_full - done) % (2 * span) >= span)
        def _(done=done, span=span):
            for u in range(span):
                step(done + u, u % 2)
        done = done + jnp.where((n_full - done) % (2 * span) >= span, span, 0)
        span //= 2
    diagonal_steps()


def _mla_attention(q, k, v, batch, seq):
    assert MLA_Q_BLOCK % (2 * MLA_K_BLOCK) == 0 and MLA_UNROLL % 2 == 0
    assert seq % MLA_Q_BLOCK == 0 and MLA_K_BLOCK % MLA_V_CHUNK == 0
    pairs = MLA_HEADS // HEADS_PER_STEP
    nq = seq // MLA_Q_BLOCK
    qk_w = HEADS_PER_STEP * HEAD_SLAB
    return pl.pallas_call(
        _mla_attn_kernel,
        grid=(batch, pairs, nq),
        in_specs=[
            pl.BlockSpec((1, MLA_Q_BLOCK, qk_w), lambda b, p, i: (b, i, p)),
            pl.BlockSpec((1, seq, qk_w), lambda b, p, i: (b, 0, p)),
            pl.BlockSpec((1, seq // MLA_V_CHUNK, HEADS_PER_STEP * MLA_V_DIM, MLA_V_CHUNK),
                         lambda b, p, i: (b, 0, p, 0)),
        ],
        out_specs=pl.BlockSpec((1, MLA_Q_BLOCK, LANES), lambda b, p, i: (b, i, p)),
        out_shape=jax.ShapeDtypeStruct((batch, seq, MLA_HEADS * MLA_V_DIM), jnp.bfloat16),
        scratch_shapes=[
            pltpu.VMEM((HEADS_PER_STEP, MLA_K_BLOCK, MLA_Q_BLOCK), jnp.float32),
            pltpu.VMEM((HEADS_PER_STEP, MLA_K_BLOCK, MLA_Q_BLOCK), jnp.float32),
            pltpu.VMEM((HEADS_PER_STEP, MLA_K_BLOCK, MLA_Q_BLOCK), jnp.bfloat16),
            pltpu.VMEM((HEADS_PER_STEP, MLA_K_BLOCK, MLA_Q_BLOCK), jnp.bfloat16),
            pltpu.VMEM((2, HEADS_PER_STEP, 1, MLA_Q_BLOCK), jnp.float32),
            pltpu.VMEM((HEADS_PER_STEP, 1, MLA_Q_BLOCK), jnp.float32),
            pltpu.VMEM((2, HEADS_PER_STEP, 1, MLA_Q_BLOCK), jnp.float32),
            pltpu.VMEM((HEADS_PER_STEP, MLA_V_DIM + MLA_DENOM_ROWS, MLA_Q_BLOCK), jnp.float32),
        ],
        compiler_params=_params(("parallel", "parallel", "arbitrary")),
        name="mla_attn",
    )(q, k, v)


def kernel(x, positions, norm_g, ffn_w_in, ffn_w_out, sb_w_in, sb_w_out, mla_w_in,
           mla_q_norm, mla_w_uq, mla_kv_norm, mla_w_ukv, mla_w_out):
    batch, seq, _ = x.shape
    t = batch * seq
    bf = jnp.bfloat16
    depth = norm_g.shape[0]
    xt = x.reshape(t, D_MODEL)
    pos = positions.reshape(t, 1)
    ffn_in = ffn_w_in.astype(bf)
    ffn_out = ffn_w_out.astype(bf)
    for i in range(depth):
        g = norm_g[i][:, None, :]
        xt = _ffn(xt, g[0], g[1], ffn_in, ffn_out, i, 0)
        j = i // 2
        if i % 2 == 0:
            width = SB_HEADS * SB_HEAD_DIM
            w_qk = sb_w_in[j][:, :2 * width].astype(bf)
            w_vt = sb_w_in[j][:, 2 * width:].T.astype(bf)
            qk, v_t = _sb_qkv(xt, g[2], w_qk, w_vt)
            o = _sb_attention(qk.reshape(batch, seq, -1),
                              v_t.reshape(batch, seq // SB_BLOCK, width, SB_BLOCK), batch, seq)
            w_out = sb_w_out[j]
        else:
            w_in_p, w_uq_p, w_uk_p, w_uv = _mla_weights(mla_w_in[j], mla_w_uq[j], mla_w_ukv[j])
            q, k, v = _mla_proj(xt, pos, g[2], w_in_p, mla_q_norm[j][None, :],
                                mla_kv_norm[j][None, :], w_uq_p, w_uk_p, w_uv)
            v_t = v.reshape(batch, seq // MLA_V_CHUNK, MLA_HEADS * MLA_V_DIM, MLA_V_CHUNK)
            o = _mla_attention(q.reshape(batch, seq, -1), k.reshape(batch, seq, -1),
                               v_t, batch, seq)
            w_out = mla_w_out[j]
        xt = _mixer_out_ffn(xt, o.reshape(t, -1), w_out.astype(bf), g[3], g[4], g[5],
                            ffn_in, ffn_out, i, 1)
    return xt.reshape(batch, seq, D_MODEL)
```

```python
import functools
import math

import jax
import jax.numpy as jnp
from jax import lax
from jax.experimental import pallas as pl
from jax.experimental.pallas import tpu as pltpu

D_MODEL = 1024
EPS = 1e-6
D_FF = 2816
FFN_RESIDUAL_WEIGHT = 0.5

SB_HEADS = 16
SB_HEAD_DIM = 64

MLA_HEADS = 16
MLA_Q_LORA = 256
MLA_KV_LORA = 128
MLA_NOPE_DIM = 64
MLA_ROPE_DIM = 32
MLA_V_DIM = 64
ROPE_THETA = 10000.0

LANES = 128
HEAD_SLAB = 128
HEADS_PER_STEP = 2
VMEM_LIMIT_BYTES = 56 * 1024 * 1024

SB_TAIL_CUTOFF = 104.0
SB_BLOCK = 128
SB_HEADS_PER_STEP = 8

MLA_Q_BLOCK = 1024
MLA_K_BLOCK = 512
MLA_UNROLL = 2
MLA_V_CHUNK = 256
MLA_SCORE_SCALE = math.log2(math.e) / math.sqrt(MLA_NOPE_DIM + MLA_ROPE_DIM)
MLA_DENOM_ROWS = 16


def _rms(x, g):
    return x * lax.rsqrt(jnp.mean(x * x, axis=-1, keepdims=True) + EPS) * g


def _dot(a, b):
    return jnp.dot(a, b, preferred_element_type=jnp.float32)


def _dot_nt(a, b):
    return lax.dot_general(a, b, (((1,), (1,)), ((), ())),
                           preferred_element_type=jnp.float32)


def _params(semantics):
    return pltpu.CompilerParams(dimension_semantics=semantics,
                                vmem_limit_bytes=VMEM_LIMIT_BYTES)


def _const_spec(shape):
    return pl.BlockSpec(shape, lambda *_: (0,) * len(shape),
                        pipeline_mode=pl.Buffered(1))


FFN_TOKENS = 512


def _ffn_block(x, gpre_ref, gpost_ref, win_ref, wout_ref):
    xn = _rms(x, gpre_ref[...]).astype(jnp.bfloat16)
    h = _dot(xn, win_ref[...])
    gate = h[:, :D_FF]
    up = h[:, D_FF:]
    act = (gate * jax.nn.sigmoid(gate) * up).astype(jnp.bfloat16)
    f = _dot(act, wout_ref[...])
    return x + FFN_RESIDUAL_WEIGHT * _rms(f, gpost_ref[...])


def _ffn_kernel(x_ref, gpre_ref, gpost_ref, win_ref, wout_ref, y_ref):
    y_ref[...] = _ffn_block(x_ref[...], gpre_ref, gpost_ref, win_ref, wout_ref)


def _mixer_out_ffn_kernel(x_ref, o_ref, wo_ref, gmix_ref, gpre_ref, gpost_ref, win_ref, wout_ref,
                          y_ref):
    x = x_ref[...] + _rms(_dot(o_ref[...], wo_ref[...]), gmix_ref[...])
    y_ref[...] = _ffn_block(x, gpre_ref, gpost_ref, win_ref, wout_ref)


def _ffn_specs(layer, half):
    def weight(rows, cols):
        return pl.BlockSpec((None, None, rows, cols), lambda *_: (layer, half, 0, 0),
                            pipeline_mode=pl.Buffered(1))
    return [
        _const_spec((1, D_MODEL)),
        _const_spec((1, D_MODEL)),
        weight(D_MODEL, 2 * D_FF),
        weight(D_FF, D_MODEL),
    ]


def _ffn(x, g_pre, g_post, w_in, w_out, layer, half):
    t = x.shape[0]
    rows = pl.BlockSpec((FFN_TOKENS, D_MODEL), lambda i: (i, 0))
    return pl.pallas_call(
        _ffn_kernel,
        grid=(t // FFN_TOKENS,),
        in_specs=[rows] + _ffn_specs(layer, half),
        out_specs=rows,
        out_shape=jax.ShapeDtypeStruct((t, D_MODEL), jnp.float32),
        compiler_params=_params(("parallel",)),
        name="ffn",
    )(x, g_pre, g_post, w_in, w_out)


def _mixer_out_ffn(x, o, w_o, g_mix, g_pre, g_post, w_in, w_out, layer, half):
    t = x.shape[0]
    rows = pl.BlockSpec((FFN_TOKENS, D_MODEL), lambda i: (i, 0))
    return pl.pallas_call(
        _mixer_out_ffn_kernel,
        grid=(t // FFN_TOKENS,),
        in_specs=[rows, rows, _const_spec((D_MODEL, D_MODEL)), _const_spec((1, D_MODEL))]
        + _ffn_specs(layer, half),
        out_specs=rows,
        out_shape=jax.ShapeDtypeStruct((t, D_MODEL), jnp.float32),
        compiler_params=_params(("parallel",)),
        name="mixer_out_ffn",
    )(x, o, w_o, g_mix, g_pre, g_post, w_in, w_out)


def _sb_qkv_kernel(x_ref, g_ref, wqk_ref, wvt_ref, qk_ref, vt_ref):
    h = _rms(x_ref[...], g_ref[...]).astype(jnp.bfloat16)
    width = SB_HEADS * SB_HEAD_DIM
    y = _dot(h, wqk_ref[...])
    qk_ref[:, :width] = (y[:, :width] * (1.0 / math.sqrt(SB_HEAD_DIM))).astype(qk_ref.dtype)
    qk_ref[:, width:] = y[:, width:].astype(qk_ref.dtype)
    v_t = _dot_nt(wvt_ref[...], h).astype(vt_ref.dtype)
    for ci in range(vt_ref.shape[0]):
        vt_ref[ci] = v_t[:, ci * SB_BLOCK:(ci + 1) * SB_BLOCK]


def _sb_qkv(x, g, w_qk, w_vt, tm=512):
    t = x.shape[0]
    width = SB_HEADS * SB_HEAD_DIM
    chunks = tm // SB_BLOCK
    return pl.pallas_call(
        _sb_qkv_kernel,
        grid=(t // tm,),
        in_specs=[
            pl.BlockSpec((tm, D_MODEL), lambda i: (i, 0)),
            _const_spec((1, D_MODEL)),
            _const_spec((D_MODEL, 2 * width)),
            _const_spec((width, D_MODEL)),
        ],
        out_specs=[
            pl.BlockSpec((tm, 2 * width), lambda i: (i, 0)),
            pl.BlockSpec((chunks, width, SB_BLOCK), lambda i: (i, 0, 0)),
        ],
        out_shape=[
            jax.ShapeDtypeStruct((t, 2 * width), jnp.bfloat16),
            jax.ShapeDtypeStruct((t // SB_BLOCK, width, SB_BLOCK), jnp.bfloat16),
        ],
        compiler_params=_params(("parallel",)),
        name="sb_qkv",
    )(x, g, w_qk, w_vt)


def _sb_attn_kernel(q_ref, k_ref, v_ref, o_ref):
    qi = pl.program_id(2)
    blk = SB_BLOCK
    heads = range(SB_HEADS_PER_STEP)
    lane = lax.broadcasted_iota(jnp.int32, (blk, LANES), 1)
    key = lax.broadcasted_iota(jnp.int32, (blk, blk), 0)
    qry = lax.broadcasted_iota(jnp.int32, (blk, blk), 1)
    causal = key < qry
    half = blk // 2

    def cum_matrix(tk):
        ur = lax.broadcasted_iota(jnp.int32, (tk, 2 * tk), 0)
        uc = lax.broadcasted_iota(jnp.int32, (tk, 2 * tk), 1)
        return jnp.where((uc % tk) > ur, 1.0, 0.0).astype(jnp.bfloat16)

    cum = {blk: cum_matrix(blk), half: cum_matrix(half)}
    slabs = [slice((h // 2) * LANES, (h // 2 + 1) * LANES) for h in heads]
    q_heads = []
    for h in heads:
        q = q_ref[0, :, slabs[h]]
        q_heads.append(jnp.where((lane // SB_HEAD_DIM) == h % 2, q, jnp.zeros_like(q)))

    def logits(start, tk, h):
        start = pl.multiple_of(start, tk)
        return _dot_nt(k_ref[0, pl.ds(start, tk), slabs[h]], q_heads[h])

    def softplus_parts(z, mask):
        sp = jnp.maximum(z, 0.0) + jnp.log(1.0 + jnp.exp2(jnp.abs(z) * (-math.log2(math.e))))
        sp_m = sp if mask is None else jnp.where(mask, sp, 0.0)
        hi = sp_m.astype(jnp.bfloat16)
        lo = (sp_m - hi.astype(jnp.float32)).astype(jnp.bfloat16)
        return sp, jnp.concatenate([hi, lo], axis=0), jnp.sum(sp_m, axis=0, keepdims=True)

    def weights(z, sp, inner, tail, mask):
        w = jnp.exp(z - sp - inner - tail)
        return (w if mask is None else jnp.where(mask, w, 0.0)).astype(jnp.bfloat16)

    def values(v_t, a, valid=None):
        if valid is not None:
            v_t = jnp.where(valid, v_t, jnp.zeros_like(v_t))
        return _dot(v_t, a)

    def head_rows(h):
        return slice(h * SB_HEAD_DIM, (h + 1) * SB_HEAD_DIM)

    b1 = jnp.maximum(qi - 1, 0)
    b2 = jnp.maximum(qi - 2, 0)
    spans = [(qi * blk, blk, qi, slice(0, blk), causal, None),
             (b1 * blk, blk, b1, slice(0, blk), None, qi >= 1),
             (b2 * blk + half, half, b2, slice(half, blk), None, qi >= 2)]
    tiles = [(d, h) for d in range(len(spans)) for h in heads]
    z = {(d, h): logits(spans[d][0], spans[d][1], h) for d, h in tiles}
    parts = {t: softplus_parts(z[t], spans[t[0]][4]) for t in tiles}
    inner = {t: _dot(cum[spans[t[0]][1]], parts[t][1]) for t in tiles}
    tails = [jnp.zeros((1, blk), jnp.float32) for _ in heads]
    a = {}
    for d, h in tiles:
        valid = spans[d][5]
        a[d, h] = weights(z[d, h], parts[d, h][0], inner[d, h], tails[h], spans[d][4])
        total = parts[d, h][2]
        tails[h] = tails[h] + (total if valid is None else jnp.where(valid, total, 0.0))
    accs = [sum(values(v_ref[0, chunk, head_rows(h), lanes], a[d, h], valid)
                for d, (_, _, chunk, lanes, _, valid) in enumerate(spans)) for h in heads]

    def live(state):
        smallest = functools.reduce(jnp.minimum, [state[1 + 2 * h] for h in heads])
        return jnp.logical_and(state[0] >= 0, jnp.min(smallest) < SB_TAIL_CUTOFF)

    def older(state):
        hb = state[0]
        newer_half = hb % 2 == 1
        out = [hb - 1]
        for h in heads:
            tail, acc = state[1 + 2 * h], state[2 + 2 * h]
            zz = logits(hb * half, half, h)
            sp, hilo, total = softplus_parts(zz, None)
            v_blk = v_ref[0, hb // 2, head_rows(h), :]
            v_t = jnp.where(newer_half, v_blk[:, half:], v_blk[:, :half])
            acc = acc + values(v_t, weights(zz, sp, _dot(cum[half], hilo), tail, None))
            out += [tail + total, acc]
        return tuple(out)

    init = [2 * qi - 4]
    for h in heads:
        init += [tails[h], accs[h]]
    state = lax.while_loop(live, older, tuple(init))
    o_t = jnp.concatenate([state[2 + 2 * h] for h in heads], axis=0)
    o_ref[0] = o_t.T.astype(o_ref.dtype)


def _sb_attention(qk, v_t, batch, seq):
    groups = SB_HEADS // SB_HEADS_PER_STEP
    width = SB_HEADS_PER_STEP * SB_HEAD_DIM
    nq = seq // SB_BLOCK
    return pl.pallas_call(
        _sb_attn_kernel,
        grid=(batch, groups, nq),
        in_specs=[
            pl.BlockSpec((1, SB_BLOCK, width), lambda b, p, i: (b, i, p)),
            pl.BlockSpec((1, seq, width), lambda b, p, i: (b, 0, groups + p)),
            pl.BlockSpec((1, nq, width, SB_BLOCK), lambda b, p, i: (b, 0, p, 0)),
        ],
        out_specs=pl.BlockSpec((1, SB_BLOCK, width), lambda b, p, i: (b, i, p)),
        out_shape=jax.ShapeDtypeStruct((batch, seq, SB_HEADS * SB_HEAD_DIM), jnp.bfloat16),
        compiler_params=_params(("parallel", "parallel", "arbitrary")),
        name="sb_attn",
    )(qk, qk, v_t)


def _mla_proj_kernel(x_ref, pos_ref, freq_ref, g_ref, win_ref, qn_ref, kvn_ref,
                     wuq_ref, wuk_ref, wuv_ref, q_ref, k_ref, v_ref):
    h = _rms(x_ref[...], g_ref[...]).astype(jnp.bfloat16)
    proj = _dot(h, win_ref[...])
    c_q = proj[:, :MLA_Q_LORA]
    c_kv = proj[:, MLA_Q_LORA:MLA_Q_LORA + MLA_KV_LORA]
    kr = proj[:, MLA_Q_LORA + MLA_KV_LORA:MLA_Q_LORA + MLA_KV_LORA + HEAD_SLAB]
    kr_rot = proj[:, MLA_Q_LORA + MLA_KV_LORA + HEAD_SLAB:]

    tm = x_ref.shape[0]
    lane = lax.broadcasted_iota(jnp.int32, (tm, HEAD_SLAB), 1)
    ang = pos_ref[...].astype(jnp.float32) * freq_ref[...]
    is_rope = jnp.logical_and(lane >= MLA_NOPE_DIM, lane < MLA_NOPE_DIM + MLA_ROPE_DIM)
    cos = jnp.where(lane < MLA_NOPE_DIM, 1.0, jnp.where(is_rope, jnp.cos(ang), 0.0))
    sin = jnp.where(is_rope, jnp.sin(ang), 0.0)

    qn = _rms(c_q, qn_ref[...]).astype(jnp.bfloat16)
    q_all = _dot(qn, wuq_ref[...])
    kvn = _rms(c_kv, kvn_ref[...]).astype(jnp.bfloat16)
    k_nope = _dot(kvn, wuk_ref[...])
    v_ref[0] = _dot_nt(wuv_ref[...], kvn).astype(v_ref.dtype)
    k_rope = kr * cos + kr_rot * sin
    cos_q = cos * MLA_SCORE_SCALE
    sin_q = sin * MLA_SCORE_SCALE
    width = MLA_HEADS * HEAD_SLAB
    for hd in range(MLA_HEADS):
        sl = slice(hd * HEAD_SLAB, (hd + 1) * HEAD_SLAB)
        rot = slice(width + hd * HEAD_SLAB, width + (hd + 1) * HEAD_SLAB)
        q_ref[:, sl] = (q_all[:, sl] * cos_q + q_all[:, rot] * sin_q).astype(q_ref.dtype)
        k_ref[:, sl] = (k_nope[:, sl] + k_rope).astype(k_ref.dtype)


def _mla_weights(w_in, w_uq, w_ukv):
    half = MLA_ROPE_DIM // 2
    pad = HEAD_SLAB - MLA_NOPE_DIM - MLA_ROPE_DIM

    def slab(nope, rope):
        return jnp.concatenate(
            [nope, rope, jnp.zeros(rope.shape[:-1] + (pad,), rope.dtype)], axis=-1)

    def rotate(rope):
        return jnp.concatenate([-rope[..., half:], rope[..., :half]], axis=-1)

    lat = MLA_Q_LORA + MLA_KV_LORA
    kr_w = w_in[:, lat:]
    zeros_in = jnp.zeros((D_MODEL, MLA_NOPE_DIM), w_in.dtype)
    w_in_p = jnp.concatenate(
        [w_in[:, :lat], slab(zeros_in, kr_w), slab(zeros_in, rotate(kr_w))], axis=1)

    uq = w_uq.reshape(MLA_Q_LORA, MLA_HEADS, MLA_NOPE_DIM + MLA_ROPE_DIM)
    uq_nope, uq_rope = uq[..., :MLA_NOPE_DIM], uq[..., MLA_NOPE_DIM:]
    w_uq_p = jnp.concatenate(
        [slab(uq_nope, uq_rope).reshape(MLA_Q_LORA, -1),
         slab(jnp.zeros_like(uq_nope), rotate(uq_rope)).reshape(MLA_Q_LORA, -1)], axis=1)

    ukv = w_ukv.reshape(MLA_KV_LORA, MLA_HEADS, MLA_NOPE_DIM + MLA_V_DIM)
    uk = ukv[..., :MLA_NOPE_DIM]
    w_uk_p = jnp.concatenate([uk, jnp.zeros_like(uk)], axis=-1).reshape(MLA_KV_LORA, -1)
    w_uv = ukv[..., MLA_NOPE_DIM:].reshape(MLA_KV_LORA, -1).T
    bf = jnp.bfloat16
    return w_in_p.astype(bf), w_uq_p.astype(bf), w_uk_p.astype(bf), w_uv.astype(bf)


def _rope_freq_slab():
    inv_freq = ROPE_THETA ** (-jnp.arange(0, MLA_ROPE_DIM, 2, dtype=jnp.float32) / MLA_ROPE_DIM)
    zeros = jnp.zeros((MLA_NOPE_DIM,), jnp.float32)
    pad = jnp.zeros((HEAD_SLAB - MLA_NOPE_DIM - MLA_ROPE_DIM,), jnp.float32)
    return jnp.concatenate([zeros, inv_freq, inv_freq, pad])[None, :]


def _mla_proj(x, pos, g, w_in_p, q_norm, kv_norm, w_uq_p, w_uk_p, w_uv):
    tm = MLA_V_CHUNK
    t = x.shape[0]
    width = MLA_HEADS * HEAD_SLAB
    vwidth = MLA_HEADS * MLA_V_DIM
    row = lambda i: (i, 0)
    return pl.pallas_call(
        _mla_proj_kernel,
        grid=(t // tm,),
        in_specs=[
            pl.BlockSpec((tm, D_MODEL), row),
            pl.BlockSpec((tm, 1), row),
            _const_spec((1, HEAD_SLAB)),
            _const_spec((1, D_MODEL)),
            _const_spec(w_in_p.shape),
            _const_spec((1, MLA_Q_LORA)),
            _const_spec((1, MLA_KV_LORA)),
            _const_spec(w_uq_p.shape),
            _const_spec(w_uk_p.shape),
            _const_spec(w_uv.shape),
        ],
        out_specs=[
            pl.BlockSpec((tm, width), row),
            pl.BlockSpec((tm, width), row),
            pl.BlockSpec((1, vwidth, tm), lambda i: (i, 0, 0)),
        ],
        out_shape=[
            jax.ShapeDtypeStruct((t, width), jnp.bfloat16),
            jax.ShapeDtypeStruct((t, width), jnp.bfloat16),
            jax.ShapeDtypeStruct((t // tm, vwidth, tm), jnp.bfloat16),
        ],
        compiler_params=_params(("parallel",)),
        name="mla_proj",
    )(x, pos, _rope_freq_slab(), g, w_in_p, q_norm, kv_norm, w_uq_p, w_uk_p, w_uv)


def _mla_attn_kernel(q_ref, k_ref, v_ref, o_ref,
                     s0_ref, s1_ref, p0_ref, p1_ref, mx_ref, m_ref, alpha_ref, acc_ref):
    qi = pl.program_id(2)
    tq, tk = MLA_Q_BLOCK, MLA_K_BLOCK
    chunks = tk // MLA_V_CHUNK
    heads = range(HEADS_PER_STEP)
    q_heads = [q_ref[0, :, h * HEAD_SLAB:(h + 1) * HEAD_SLAB] for h in heads]
    ones = jnp.ones((MLA_DENOM_ROWS, tk), jnp.bfloat16)
    ratio = tq // tk
    n_full = qi * ratio
    s_bufs = (s0_ref, s1_ref)
    p_bufs = (p0_ref, p1_ref)

    def scores(j, parity, first=0):
        start = pl.multiple_of(j * tk, tk)
        for h in heads:
            k_blk = k_ref[0, pl.ds(start, tk), h * HEAD_SLAB:(h + 1) * HEAD_SLAB]
            s = _dot_nt(k_blk, q_heads[h][first:])
            s_bufs[parity][h, :, first:] = s
            mx_ref[parity, h, :, first:] = jnp.max(s, axis=0, keepdims=True)

    def values(j, parity, first=0):
        for h in heads:
            v_t = jnp.concatenate(
                [v_ref[0, j * chunks + ci, h * MLA_V_DIM:(h + 1) * MLA_V_DIM, :]
                 for ci in range(chunks)], axis=1)
            pv = _dot(jnp.concatenate([v_t, ones], axis=0), p_bufs[parity][h, :, first:])
            acc_ref[h, :, first:] = alpha_ref[parity, h, :, first:] * acc_ref[h, :, first:] + pv

    def softmax(parity, mask, first=0):
        for h in heads:
            s = s_bufs[parity][h, :, first:]
            if mask is None:
                block_max = mx_ref[parity, h, :, first:]
            else:
                s = jnp.where(mask, s, -jnp.inf)
                block_max = jnp.max(s, axis=0, keepdims=True)
            m_old = m_ref[h, :, first:]
            m_new = jnp.maximum(m_old, block_max)
            p_bufs[parity][h, :, first:] = jnp.exp2(s - m_new).astype(jnp.bfloat16)
            alpha_ref[parity, h, :, first:] = jnp.exp2(m_old - m_new)
            m_ref[h, :, first:] = m_new

    def step(i, parity):
        softmax(parity, None)
        scores(i + 1, 1 - parity)
        values(jnp.maximum(i - 1, 0), 1 - parity)

    def diagonal_steps():
        for off in range(ratio):
            if off + 1 < ratio:
                scores(n_full + off + 1, (off + 1) % 2, (off + 1) * tk)
            key = lax.broadcasted_iota(jnp.int32, (tk, tq - off * tk), 0)
            qry = lax.broadcasted_iota(jnp.int32, (tk, tq - off * tk), 1)
            softmax(off % 2, key <= qry, off * tk)
            values(jnp.maximum(n_full + off - 1, 0), (off + 1) % 2, max(off - 1, 0) * tk)
        values(n_full + ratio - 1, (ratio - 1) % 2, (ratio - 1) * tk)
        o_t = jnp.concatenate(
            [acc_ref[h, :MLA_V_DIM] / acc_ref[h, MLA_V_DIM:MLA_V_DIM + 1] for h in heads], axis=0)
        o_ref[0] = o_t.T.astype(o_ref.dtype)

    m_ref[...] = jnp.full(m_ref.shape, -jnp.inf, jnp.float32)
    alpha_ref[...] = jnp.ones(alpha_ref.shape, jnp.float32)
    acc_ref[...] = jnp.zeros(acc_ref.shape, jnp.float32)
    p1_ref[...] = jnp.zeros(p1_ref.shape, jnp.bfloat16)
    scores(0, 0)

    def unrolled_steps(ii, carry):
        for u in range(MLA_UNROLL):
            step(MLA_UNROLL * ii + u, u % 2)
        return carry

    lax.fori_loop(0, n_full // MLA_UNROLL, unrolled_steps, 0)
    done = (n_full // MLA_UNROLL) * MLA_UNROLL
    span = MLA_UNROLL // 2
    while span >= 2:
        @pl.when((n_full - done) % (2 * span) >= span)
        def _(done=done, span=span):
            for u in range(span):
                step(done + u, u % 2)
        done = done + jnp.where((n_full - done) % (2 * span) >= span, span, 0)
        span //= 2
    diagonal_steps()


def _mla_attention(q, k, v, batch, seq):
    assert MLA_Q_BLOCK % (2 * MLA_K_BLOCK) == 0 and MLA_UNROLL % 2 == 0
    assert seq % MLA_Q_BLOCK == 0 and MLA_K_BLOCK % MLA_V_CHUNK == 0
    pairs = MLA_HEADS // HEADS_PER_STEP
    nq = seq // MLA_Q_BLOCK
    qk_w = HEADS_PER_STEP * HEAD_SLAB
    return pl.pallas_call(
        _mla_attn_kernel,
        grid=(batch, pairs, nq),
        in_specs=[
            pl.BlockSpec((1, MLA_Q_BLOCK, qk_w), lambda b, p, i: (b, i, p)),
            pl.BlockSpec((1, seq, qk_w), lambda b, p, i: (b, 0, p)),
            pl.BlockSpec((1, seq // MLA_V_CHUNK, HEADS_PER_STEP * MLA_V_DIM, MLA_V_CHUNK),
                         lambda b, p, i: (b, 0, p, 0)),
        ],
        out_specs=pl.BlockSpec((1, MLA_Q_BLOCK, LANES), lambda b, p, i: (b, i, p)),
        out_shape=jax.ShapeDtypeStruct((batch, seq, MLA_HEADS * MLA_V_DIM), jnp.bfloat16),
        scratch_shapes=[
            pltpu.VMEM((HEADS_PER_STEP, MLA_K_BLOCK, MLA_Q_BLOCK), jnp.float32),
            pltpu.VMEM((HEADS_PER_STEP, MLA_K_BLOCK, MLA_Q_BLOCK), jnp.float32),
            pltpu.VMEM((HEADS_PER_STEP, MLA_K_BLOCK, MLA_Q_BLOCK), jnp.bfloat16),
            pltpu.VMEM((HEADS_PER_STEP, MLA_K_BLOCK, MLA_Q_BLOCK), jnp.bfloat16),
            pltpu.VMEM((2, HEADS_PER_STEP, 1, MLA_Q_BLOCK), jnp.float32),
            pltpu.VMEM((HEADS_PER_STEP, 1, MLA_Q_BLOCK), jnp.float32),
            pltpu.VMEM((2, HEADS_PER_STEP, 1, MLA_Q_BLOCK), jnp.float32),
            pltpu.VMEM((HEADS_PER_STEP, MLA_V_DIM + MLA_DENOM_ROWS, MLA_Q_BLOCK), jnp.float32),
        ],
        compiler_params=_params(("parallel", "parallel", "arbitrary")),
        name="mla_attn",
    )(q, k, v)


def kernel(x, positions, norm_g, ffn_w_in, ffn_w_out, sb_w_in, sb_w_out, mla_w_in,
           mla_q_norm, mla_w_uq, mla_kv_norm, mla_w_ukv, mla_w_out):
    batch, seq, _ = x.shape
    t = batch * seq
    bf = jnp.bfloat16
    depth = norm_g.shape[0]
    xt = x.reshape(t, D_MODEL)
    pos = positions.reshape(t, 1)
    ffn_in = ffn_w_in.astype(bf)
    ffn_out = ffn_w_out.astype(bf)
    for i in range(depth):
        g = norm_g[i][:, None, :]
        xt = _ffn(xt, g[0], g[1], ffn_in, ffn_out, i, 0)
        j = i // 2
        if i % 2 == 0:
            width = SB_HEADS * SB_HEAD_DIM
            w_qk = sb_w_in[j][:, :2 * width].astype(bf)
            w_vt = sb_w_in[j][:, 2 * width:].T.astype(bf)
            qk, v_t = _sb_qkv(xt, g[2], w_qk, w_vt)
            o = _sb_attention(qk.reshape(batch, seq, -1),
                              v_t.reshape(batch, seq // SB_BLOCK, width, SB_BLOCK), batch, seq)
            w_out = sb_w_out[j]
        else:
            w_in_p, w_uq_p, w_uk_p, w_uv = _mla_weights(mla_w_in[j], mla_w_uq[j], mla_w_ukv[j])
            q, k, v = _mla_proj(xt, pos, g[2], w_in_p, mla_q_norm[j][None, :],
                                mla_kv_norm[j][None, :], w_uq_p, w_uk_p, w_uv)
            v_t = v.reshape(batch, seq // MLA_V_CHUNK, MLA_HEADS * MLA_V_DIM, MLA_V_CHUNK)
            o = _mla_attention(q.reshape(batch, seq, -1), k.reshape(batch, seq, -1),
                               v_t, batch, seq)
            w_out = mla_w_out[j]
        xt = _mixer_out_ffn(xt, o.reshape(t, -1), w_out.astype(bf), g[3], g[4], g[5],
                            ffn_in, ffn_out, i, 1)
    return xt.reshape(batch, seq, D_MODEL)
```

```python
import functools
import math

import jax
import jax.numpy as jnp
from jax import lax
from jax.experimental import pallas as pl
from jax.experimental.pallas import tpu as pltpu

D_MODEL = 1024
EPS = 1e-6
D_FF = 2816
FFN_RESIDUAL_WEIGHT = 0.5

SB_HEADS = 16
SB_HEAD_DIM = 64

MLA_HEADS = 16
MLA_Q_LORA = 256
MLA_KV_LORA = 128
MLA_NOPE_DIM = 64
MLA_ROPE_DIM = 32
MLA_V_DIM = 64
ROPE_THETA = 10000.0

LANES = 128
HEAD_SLAB = 128
HEADS_PER_STEP = 2
VMEM_LIMIT_BYTES = 56 * 1024 * 1024

SB_TAIL_CUTOFF = 104.0
SB_BLOCK = 128
SB_HEADS_PER_STEP = 8

MLA_Q_BLOCK = 1024
MLA_K_BLOCK = 512
MLA_PROJ_TOKENS = 512
MLA_V_CHUNK = 256
MLA_SCORE_SCALE = math.log2(math.e) / math.sqrt(MLA_NOPE_DIM + MLA_ROPE_DIM)
MLA_DENOM_ROWS = 16


def _rms(x, g):
    return x * lax.rsqrt(jnp.mean(x * x, axis=-1, keepdims=True) + EPS) * g


def _dot(a, b):
    return jnp.dot(a, b, preferred_element_type=jnp.float32)


def _dot_nt(a, b):
    return lax.dot_general(a, b, (((1,), (1,)), ((), ())),
                           preferred_element_type=jnp.float32)


def _params(semantics):
    return pltpu.CompilerParams(dimension_semantics=semantics,
                                vmem_limit_bytes=VMEM_LIMIT_BYTES)


def _const_spec(shape):
    return pl.BlockSpec(shape, lambda *_: (0,) * len(shape),
                        pipeline_mode=pl.Buffered(1))


FFN_TOKENS = 512


def _ffn_block(x, gpre_ref, gpost_ref, win_ref, wout_ref):
    xn = _rms(x, gpre_ref[...]).astype(jnp.bfloat16)
    h = _dot(xn, win_ref[...])
    gate = h[:, :D_FF]
    up = h[:, D_FF:]
    act = (gate * jax.nn.sigmoid(gate) * up).astype(jnp.bfloat16)
    f = _dot(act, wout_ref[...])
    return x + FFN_RESIDUAL_WEIGHT * _rms(f, gpost_ref[...])


def _ffn_kernel(x_ref, gpre_ref, gpost_ref, win_ref, wout_ref, y_ref):
    y_ref[...] = _ffn_block(x_ref[...], gpre_ref, gpost_ref, win_ref, wout_ref)


def _mixer_out_ffn_kernel(x_ref, o_ref, wo_ref, gmix_ref, gpre_ref, gpost_ref, win_ref, wout_ref,
                          y_ref):
    x = x_ref[...] + _rms(_dot(o_ref[...], wo_ref[...]), gmix_ref[...])
    y_ref[...] = _ffn_block(x, gpre_ref, gpost_ref, win_ref, wout_ref)


def _ffn_specs(layer, half):
    def weight(rows, cols):
        return pl.BlockSpec((None, None, rows, cols), lambda *_: (layer, half, 0, 0),
                            pipeline_mode=pl.Buffered(1))
    return [
        _const_spec((1, D_MODEL)),
        _const_spec((1, D_MODEL)),
        weight(D_MODEL, 2 * D_FF),
        weight(D_FF, D_MODEL),
    ]


def _ffn(x, g_pre, g_post, w_in, w_out, layer, half):
    t = x.shape[0]
    rows = pl.BlockSpec((FFN_TOKENS, D_MODEL), lambda i: (i, 0))
    return pl.pallas_call(
        _ffn_kernel,
        grid=(t // FFN_TOKENS,),
        in_specs=[rows] + _ffn_specs(layer, half),
        out_specs=rows,
        out_shape=jax.ShapeDtypeStruct((t, D_MODEL), jnp.float32),
        compiler_params=_params(("parallel",)),
        name="ffn",
    )(x, g_pre, g_post, w_in, w_out)


def _mixer_out_ffn(x, o, w_o, g_mix, g_pre, g_post, w_in, w_out, layer, half):
    t = x.shape[0]
    rows = pl.BlockSpec((FFN_TOKENS, D_MODEL), lambda i: (i, 0))
    return pl.pallas_call(
        _mixer_out_ffn_kernel,
        grid=(t // FFN_TOKENS,),
        in_specs=[rows, rows, _const_spec((D_MODEL, D_MODEL)), _const_spec((1, D_MODEL))]
        + _ffn_specs(layer, half),
        out_specs=rows,
        out_shape=jax.ShapeDtypeStruct((t, D_MODEL), jnp.float32),
        compiler_params=_params(("parallel",)),
        name="mixer_out_ffn",
    )(x, o, w_o, g_mix, g_pre, g_post, w_in, w_out)


def _sb_qkv_kernel(x_ref, g_ref, wqk_ref, wvt_ref, qk_ref, vt_ref):
    h = _rms(x_ref[...], g_ref[...]).astype(jnp.bfloat16)
    width = SB_HEADS * SB_HEAD_DIM
    y = _dot(h, wqk_ref[...])
    qk_ref[:, :width] = (y[:, :width] * (1.0 / math.sqrt(SB_HEAD_DIM))).astype(qk_ref.dtype)
    qk_ref[:, width:] = y[:, width:].astype(qk_ref.dtype)
    v_t = _dot_nt(wvt_ref[...], h).astype(vt_ref.dtype)
    for ci in range(vt_ref.shape[0]):
        vt_ref[ci] = v_t[:, ci * SB_BLOCK:(ci + 1) * SB_BLOCK]


def _sb_qkv(x, g, w_qk, w_vt, tm=512):
    t = x.shape[0]
    width = SB_HEADS * SB_HEAD_DIM
    chunks = tm // SB_BLOCK
    return pl.pallas_call(
        _sb_qkv_kernel,
        grid=(t // tm,),
        in_specs=[
            pl.BlockSpec((tm, D_MODEL), lambda i: (i, 0)),
            _const_spec((1, D_MODEL)),
            _const_spec((D_MODEL, 2 * width)),
            _const_spec((width, D_MODEL)),
        ],
        out_specs=[
            pl.BlockSpec((tm, 2 * width), lambda i: (i, 0)),
            pl.BlockSpec((chunks, width, SB_BLOCK), lambda i: (i, 0, 0)),
        ],
        out_shape=[
            jax.ShapeDtypeStruct((t, 2 * width), jnp.bfloat16),
            jax.ShapeDtypeStruct((t // SB_BLOCK, width, SB_BLOCK), jnp.bfloat16),
        ],
        compiler_params=_params(("parallel",)),
        name="sb_qkv",
    )(x, g, w_qk, w_vt)


def _sb_attn_kernel(q_ref, k_ref, v_ref, o_ref):
    qi = pl.program_id(2)
    blk = SB_BLOCK
    heads = range(SB_HEADS_PER_STEP)
    lane = lax.broadcasted_iota(jnp.int32, (blk, LANES), 1)
    key = lax.broadcasted_iota(jnp.int32, (blk, blk), 0)
    qry = lax.broadcasted_iota(jnp.int32, (blk, blk), 1)
    causal = key < qry
    half = blk // 2

    def cum_matrix(tk):
        ur = lax.broadcasted_iota(jnp.int32, (tk, 2 * tk), 0)
        uc = lax.broadcasted_iota(jnp.int32, (tk, 2 * tk), 1)
        return jnp.where((uc % tk) > ur, 1.0, 0.0).astype(jnp.bfloat16)

    cum = {blk: cum_matrix(blk), half: cum_matrix(half)}
    slabs = [slice((h // 2) * LANES, (h // 2 + 1) * LANES) for h in heads]
    q_heads = []
    for h in heads:
        q = q_ref[0, :, slabs[h]]
        q_heads.append(jnp.where((lane // SB_HEAD_DIM) == h % 2, q, jnp.zeros_like(q)))

    def logits(start, tk, h):
        start = pl.multiple_of(start, tk)
        return _dot_nt(k_ref[0, pl.ds(start, tk), slabs[h]], q_heads[h])

    def softplus_parts(z, mask):
        sp = jnp.maximum(z, 0.0) + jnp.log(1.0 + jnp.exp2(jnp.abs(z) * (-math.log2(math.e))))
        sp_m = sp if mask is None else jnp.where(mask, sp, 0.0)
        hi = sp_m.astype(jnp.bfloat16)
        lo = (sp_m - hi.astype(jnp.float32)).astype(jnp.bfloat16)
        return sp, jnp.concatenate([hi, lo], axis=0), jnp.sum(sp_m, axis=0, keepdims=True)

    def weights(z, sp, inner, tail, mask):
        w = jnp.exp(z - sp - inner - tail)
        return (w if mask is None else jnp.where(mask, w, 0.0)).astype(jnp.bfloat16)

    def values(v_t, a, valid=None):
        if valid is not None:
            v_t = jnp.where(valid, v_t, jnp.zeros_like(v_t))
        return _dot(v_t, a)

    def head_rows(h):
        return slice(h * SB_HEAD_DIM, (h + 1) * SB_HEAD_DIM)

    b1 = jnp.maximum(qi - 1, 0)
    b2 = jnp.maximum(qi - 2, 0)
    spans = [(qi * blk, blk, qi, slice(0, blk), causal, None),
             (b1 * blk, blk, b1, slice(0, blk), None, qi >= 1),
             (b2 * blk + half, half, b2, slice(half, blk), None, qi >= 2)]
    tiles = [(d, h) for d in range(len(spans)) for h in heads]
    z = {(d, h): logits(spans[d][0], spans[d][1], h) for d, h in tiles}
    parts = {t: softplus_parts(z[t], spans[t[0]][4]) for t in tiles}
    inner = {t: _dot(cum[spans[t[0]][1]], parts[t][1]) for t in tiles}
    tails = [jnp.zeros((1, blk), jnp.float32) for _ in heads]
    a = {}
    for d, h in tiles:
        valid = spans[d][5]
        a[d, h] = weights(z[d, h], parts[d, h][0], inner[d, h], tails[h], spans[d][4])
        total = parts[d, h][2]
        tails[h] = tails[h] + (total if valid is None else jnp.where(valid, total, 0.0))
    accs = [sum(values(v_ref[0, chunk, head_rows(h), lanes], a[d, h], valid)
                for d, (_, _, chunk, lanes, _, valid) in enumerate(spans)) for h in heads]

    def live(state):
        smallest = functools.reduce(jnp.minimum, [state[1 + 2 * h] for h in heads])
        return jnp.logical_and(state[0] >= 0, jnp.min(smallest) < SB_TAIL_CUTOFF)

    def older(state):
        hb = state[0]
        newer_half = hb % 2 == 1
        out = [hb - 1]
        for h in heads:
            tail, acc = state[1 + 2 * h], state[2 + 2 * h]
            zz = logits(hb * half, half, h)
            sp, hilo, total = softplus_parts(zz, None)
            v_blk = v_ref[0, hb // 2, head_rows(h), :]
            v_t = jnp.where(newer_half, v_blk[:, half:], v_blk[:, :half])
            acc = acc + values(v_t, weights(zz, sp, _dot(cum[half], hilo), tail, None))
            out += [tail + total, acc]
        return tuple(out)

    init = [2 * qi - 4]
    for h in heads:
        init += [tails[h], accs[h]]
    state = lax.while_loop(live, older, tuple(init))
    o_t = jnp.concatenate([state[2 + 2 * h] for h in heads], axis=0)
    o_ref[0] = o_t.T.astype(o_ref.dtype)


def _sb_attention(qk, v_t, batch, seq):
    groups = SB_HEADS // SB_HEADS_PER_STEP
    width = SB_HEADS_PER_STEP * SB_HEAD_DIM
    nq = seq // SB_BLOCK
    return pl.pallas_call(
        _sb_attn_kernel,
        grid=(batch, groups, nq),
        in_specs=[
            pl.BlockSpec((1, SB_BLOCK, width), lambda b, p, i: (b, i, p)),
            pl.BlockSpec((1, seq, width), lambda b, p, i: (b, 0, groups + p)),
            pl.BlockSpec((1, nq, width, SB_BLOCK), lambda b, p, i: (b, 0, p, 0)),
        ],
        out_specs=pl.BlockSpec((1, SB_BLOCK, width), lambda b, p, i: (b, i, p)),
        out_shape=jax.ShapeDtypeStruct((batch, seq, SB_HEADS * SB_HEAD_DIM), jnp.bfloat16),
        compiler_params=_params(("parallel", "parallel", "arbitrary")),
        name="sb_attn",
    )(qk, qk, v_t)


def _mla_proj_kernel(x_ref, pos_ref, freq_ref, g_ref, win_ref, qn_ref, kvn_ref,
                     wuq_ref, wuk_ref, wuv_ref, q_ref, k_ref, v_ref):
    h = _rms(x_ref[...], g_ref[...]).astype(jnp.bfloat16)
    proj = _dot(h, win_ref[...])
    c_q = proj[:, :MLA_Q_LORA]
    c_kv = proj[:, MLA_Q_LORA:MLA_Q_LORA + MLA_KV_LORA]
    kr = proj[:, MLA_Q_LORA + MLA_KV_LORA:MLA_Q_LORA + MLA_KV_LORA + HEAD_SLAB]
    kr_rot = proj[:, MLA_Q_LORA + MLA_KV_LORA + HEAD_SLAB:]

    tm = x_ref.shape[0]
    lane = lax.broadcasted_iota(jnp.int32, (tm, HEAD_SLAB), 1)
    ang = pos_ref[...].astype(jnp.float32) * freq_ref[...]
    is_rope = jnp.logical_and(lane >= MLA_NOPE_DIM, lane < MLA_NOPE_DIM + MLA_ROPE_DIM)
    cos = jnp.where(lane < MLA_NOPE_DIM, 1.0, jnp.where(is_rope, jnp.cos(ang), 0.0))
    sin = jnp.where(is_rope, jnp.sin(ang), 0.0)

    qn = _rms(c_q, qn_ref[...]).astype(jnp.bfloat16)
    q_all = _dot(qn, wuq_ref[...])
    kvn = _rms(c_kv, kvn_ref[...]).astype(jnp.bfloat16)
    k_nope = _dot(kvn, wuk_ref[...])
    v_t = _dot_nt(wuv_ref[...], kvn).astype(v_ref.dtype)
    for ci in range(v_ref.shape[0]):
        v_ref[ci] = v_t[:, ci * MLA_V_CHUNK:(ci + 1) * MLA_V_CHUNK]
    k_rope = kr * cos + kr_rot * sin
    cos_q = cos * MLA_SCORE_SCALE
    sin_q = sin * MLA_SCORE_SCALE
    width = MLA_HEADS * HEAD_SLAB
    for hd in range(MLA_HEADS):
        sl = slice(hd * HEAD_SLAB, (hd + 1) * HEAD_SLAB)
        rot = slice(width + hd * HEAD_SLAB, width + (hd + 1) * HEAD_SLAB)
        q_ref[:, sl] = (q_all[:, sl] * cos_q + q_all[:, rot] * sin_q).astype(q_ref.dtype)
        k_ref[:, sl] = (k_nope[:, sl] + k_rope).astype(k_ref.dtype)


def _mla_weights(w_in, w_uq, w_ukv):
    half = MLA_ROPE_DIM // 2
    pad = HEAD_SLAB - MLA_NOPE_DIM - MLA_ROPE_DIM

    def slab(nope, rope):
        return jnp.concatenate(
            [nope, rope, jnp.zeros(rope.shape[:-1] + (pad,), rope.dtype)], axis=-1)

    def rotate(rope):
        return jnp.concatenate([-rope[..., half:], rope[..., :half]], axis=-1)

    lat = MLA_Q_LORA + MLA_KV_LORA
    kr_w = w_in[:, lat:]
    zeros_in = jnp.zeros((D_MODEL, MLA_NOPE_DIM), w_in.dtype)
    w_in_p = jnp.concatenate(
        [w_in[:, :lat], slab(zeros_in, kr_w), slab(zeros_in, rotate(kr_w))], axis=1)

    uq = w_uq.reshape(MLA_Q_LORA, MLA_HEADS, MLA_NOPE_DIM + MLA_ROPE_DIM)
    uq_nope, uq_rope = uq[..., :MLA_NOPE_DIM], uq[..., MLA_NOPE_DIM:]
    w_uq_p = jnp.concatenate(
        [slab(uq_nope, uq_rope).reshape(MLA_Q_LORA, -1),
         slab(jnp.zeros_like(uq_nope), rotate(uq_rope)).reshape(MLA_Q_LORA, -1)], axis=1)

    ukv = w_ukv.reshape(MLA_KV_LORA, MLA_HEADS, MLA_NOPE_DIM + MLA_V_DIM)
    uk = ukv[..., :MLA_NOPE_DIM]
    w_uk_p = jnp.concatenate([uk, jnp.zeros_like(uk)], axis=-1).reshape(MLA_KV_LORA, -1)
    w_uv = ukv[..., MLA_NOPE_DIM:].reshape(MLA_KV_LORA, -1).T
    bf = jnp.bfloat16
    return w_in_p.astype(bf), w_uq_p.astype(bf), w_uk_p.astype(bf), w_uv.astype(bf)


def _rope_freq_slab():
    inv_freq = ROPE_THETA ** (-jnp.arange(0, MLA_ROPE_DIM, 2, dtype=jnp.float32) / MLA_ROPE_DIM)
    zeros = jnp.zeros((MLA_NOPE_DIM,), jnp.float32)
    pad = jnp.zeros((HEAD_SLAB - MLA_NOPE_DIM - MLA_ROPE_DIM,), jnp.float32)
    return jnp.concatenate([zeros, inv_freq, inv_freq, pad])[None, :]


def _mla_proj(x, pos, g, w_in_p, q_norm, kv_norm, w_uq_p, w_uk_p, w_uv):
    tm = MLA_PROJ_TOKENS
    chunks = tm // MLA_V_CHUNK
    t = x.shape[0]
    width = MLA_HEADS * HEAD_SLAB
    vwidth = MLA_HEADS * MLA_V_DIM
    row = lambda i: (i, 0)
    return pl.pallas_call(
        _mla_proj_kernel,
        grid=(t // tm,),
        in_specs=[
            pl.BlockSpec((tm, D_MODEL), row),
            pl.BlockSpec((tm, 1), row),
            _const_spec((1, HEAD_SLAB)),
            _const_spec((1, D_MODEL)),
            _const_spec(w_in_p.shape),
            _const_spec((1, MLA_Q_LORA)),
            _const_spec((1, MLA_KV_LORA)),
            _const_spec(w_uq_p.shape),
            _const_spec(w_uk_p.shape),
            _const_spec(w_uv.shape),
        ],
        out_specs=[
            pl.BlockSpec((tm, width), row),
            pl.BlockSpec((tm, width), row),
            pl.BlockSpec((chunks, vwidth, MLA_V_CHUNK), lambda i: (i, 0, 0)),
        ],
        out_shape=[
            jax.ShapeDtypeStruct((t, width), jnp.bfloat16),
            jax.ShapeDtypeStruct((t, width), jnp.bfloat16),
            jax.ShapeDtypeStruct((t // MLA_V_CHUNK, vwidth, MLA_V_CHUNK), jnp.bfloat16),
        ],
        compiler_params=_params(("parallel",)),
        name="mla_proj",
    )(x, pos, _rope_freq_slab(), g, w_in_p, q_norm, kv_norm, w_uq_p, w_uk_p, w_uv)


def _mla_attn_kernel(q_ref, k_ref, v_ref, o_ref,
                     s0_ref, s1_ref, p0_ref, p1_ref, mx_ref, m_ref, alpha_ref, acc_ref):
    qi = pl.program_id(2)
    tq, tk = MLA_Q_BLOCK, MLA_K_BLOCK
    chunks = tk // MLA_V_CHUNK
    heads = range(HEADS_PER_STEP)
    q_heads = [q_ref[0, :, h * HEAD_SLAB:(h + 1) * HEAD_SLAB] for h in heads]
    ones = jnp.ones((MLA_DENOM_ROWS, tk), jnp.bfloat16)
    ratio = tq // tk
    n_full = qi * ratio
    s_bufs = (s0_ref, s1_ref)
    p_bufs = (p0_ref, p1_ref)

    def scores(j, parity, first=0):
        start = pl.multiple_of(j * tk, tk)
        for h in heads:
            k_blk = k_ref[0, pl.ds(start, tk), h * HEAD_SLAB:(h + 1) * HEAD_SLAB]
            s = _dot_nt(k_blk, q_heads[h][first:])
            s_bufs[parity][h, :, first:tq] = s
            mx_ref[parity, h, :, first:] = jnp.max(s, axis=0, keepdims=True)

    def values(j, parity, first=0):
        for h in heads:
            v_t = jnp.concatenate(
                [v_ref[0, j * chunks + ci, h * MLA_V_DIM:(h + 1) * MLA_V_DIM, :]
                 for ci in range(chunks)], axis=1)
            pv = _dot(jnp.concatenate([v_t, ones], axis=0), p_bufs[parity][h, :, first:tq])
            acc_ref[h, :, first:] = alpha_ref[parity, h, :, first:] * acc_ref[h, :, first:] + pv

    def softmax(parity, mask, first=0, last=tq):
        cols = slice(first, last)
        for h in heads:
            s = s_bufs[parity][h, :, cols]
            if mask is None:
                block_max = mx_ref[parity, h, :, cols]
            else:
                s = jnp.where(mask, s, -jnp.inf)
                block_max = jnp.max(s, axis=0, keepdims=True)
            m_old = m_ref[h, :, cols]
            m_new = jnp.maximum(m_old, block_max)
            p_bufs[parity][h, :, cols] = jnp.exp2(s - m_new).astype(jnp.bfloat16)
            alpha_ref[parity, h, :, cols] = jnp.exp2(m_old - m_new)
            m_ref[h, :, cols] = m_new

    def step(i, parity):
        softmax(parity, None)
        scores(i + 1, 1 - parity)
        values(jnp.maximum(i - 1, 0), 1 - parity)

    def diagonal_steps():
        key = lax.broadcasted_iota(jnp.int32, (tk, tk), 0)
        qry = lax.broadcasted_iota(jnp.int32, (tk, tk), 1)
        for off in range(ratio):
            if off + 1 < ratio:
                scores(n_full + off + 1, (off + 1) % 2, (off + 1) * tk)
            softmax(off % 2, key <= qry, off * tk, (off + 1) * tk)
            if off + 1 < ratio:
                softmax(off % 2, None, (off + 1) * tk)
            values(jnp.maximum(n_full + off - 1, 0), (off + 1) % 2, max(off - 1, 0) * tk)
        values(n_full + ratio - 1, (ratio - 1) % 2, (ratio - 1) * tk)
        o_t = jnp.concatenate(
            [acc_ref[h, :MLA_V_DIM] / acc_ref[h, MLA_V_DIM:MLA_V_DIM + 1] for h in heads], axis=0)
        o_ref[0] = o_t.T.astype(o_ref.dtype)

    m_ref[...] = jnp.full(m_ref.shape, -jnp.inf, jnp.float32)
    alpha_ref[...] = jnp.ones(alpha_ref.shape, jnp.float32)
    acc_ref[...] = jnp.zeros(acc_ref.shape, jnp.float32)

    p1_ref[...] = jnp.zeros(p1_ref.shape, jnp.bfloat16)
    scores(0, 0)

    def two_steps(ii, carry):
        step(2 * ii, 0)
        step(2 * ii + 1, 1)
        return carry

    lax.fori_loop(0, n_full // 2, two_steps, 0)
    diagonal_steps()


def _mla_attention(q, k, v, batch, seq):
    assert MLA_Q_BLOCK % (2 * MLA_K_BLOCK) == 0
    assert seq % MLA_Q_BLOCK == 0 and MLA_K_BLOCK % MLA_V_CHUNK == 0
    pairs = MLA_HEADS // HEADS_PER_STEP
    nq = seq // MLA_Q_BLOCK
    qk_w = HEADS_PER_STEP * HEAD_SLAB
    return pl.pallas_call(
        _mla_attn_kernel,
        grid=(batch, pairs, nq),
        in_specs=[
            pl.BlockSpec((1, MLA_Q_BLOCK, qk_w), lambda b, p, i: (b, i, p)),
            pl.BlockSpec((1, seq, qk_w), lambda b, p, i: (b, 0, p)),
            pl.BlockSpec((1, seq // MLA_V_CHUNK, HEADS_PER_STEP * MLA_V_DIM, MLA_V_CHUNK),
                         lambda b, p, i: (b, 0, p, 0)),
        ],
        out_specs=pl.BlockSpec((1, MLA_Q_BLOCK, LANES), lambda b, p, i: (b, i, p)),
        out_shape=jax.ShapeDtypeStruct((batch, seq, MLA_HEADS * MLA_V_DIM), jnp.bfloat16),
        scratch_shapes=[
            pltpu.VMEM((HEADS_PER_STEP, MLA_K_BLOCK, MLA_Q_BLOCK), jnp.float32),
            pltpu.VMEM((HEADS_PER_STEP, MLA_K_BLOCK, MLA_Q_BLOCK), jnp.float32),
            pltpu.VMEM((HEADS_PER_STEP, MLA_K_BLOCK, MLA_Q_BLOCK), jnp.bfloat16),
            pltpu.VMEM((HEADS_PER_STEP, MLA_K_BLOCK, MLA_Q_BLOCK), jnp.bfloat16),
            pltpu.VMEM((2, HEADS_PER_STEP, 1, MLA_Q_BLOCK), jnp.float32),
            pltpu.VMEM((HEADS_PER_STEP, 1, MLA_Q_BLOCK), jnp.float32),
            pltpu.VMEM((2, HEADS_PER_STEP, 1, MLA_Q_BLOCK), jnp.float32),
            pltpu.VMEM((HEADS_PER_STEP, MLA_V_DIM + MLA_DENOM_ROWS, MLA_Q_BLOCK), jnp.float32),
        ],
        compiler_params=_params(("parallel", "parallel", "arbitrary")),
        name="mla_attn",
    )(q, k, v)


def kernel(x, positions, norm_g, ffn_w_in, ffn_w_out, sb_w_in, sb_w_out, mla_w_in,
           mla_q_norm, mla_w_uq, mla_kv_norm, mla_w_ukv, mla_w_out):
    batch, seq, _ = x.shape
    t = batch * seq
    bf = jnp.bfloat16
    depth = norm_g.shape[0]
    xt = x.reshape(t, D_MODEL)
    pos = positions.reshape(t, 1)
    ffn_in = ffn_w_in.astype(bf)
    ffn_out = ffn_w_out.astype(bf)
    for i in range(depth):
        g = norm_g[i][:, None, :]
        xt = _ffn(xt, g[0], g[1], ffn_in, ffn_out, i, 0)
        j = i // 2
        if i % 2 == 0:
            width = SB_HEADS * SB_HEAD_DIM
            w_qk = sb_w_in[j][:, :2 * width].astype(bf)
            w_vt = sb_w_in[j][:, 2 * width:].T.astype(bf)
            qk, v_t = _sb_qkv(xt, g[2], w_qk, w_vt)
            o = _sb_attention(qk.reshape(batch, seq, -1),
                              v_t.reshape(batch, seq // SB_BLOCK, width, SB_BLOCK), batch, seq)
            w_out = sb_w_out[j]
        else:
            w_in_p, w_uq_p, w_uk_p, w_uv = _mla_weights(mla_w_in[j], mla_w_uq[j], mla_w_ukv[j])
            q, k, v = _mla_proj(xt, pos, g[2], w_in_p, mla_q_norm[j][None, :],
                                mla_kv_norm[j][None, :], w_uq_p, w_uk_p, w_uv)
            v_t = v.reshape(batch, seq // MLA_V_CHUNK, MLA_HEADS * MLA_V_DIM, MLA_V_CHUNK)
            o = _mla_attention(q.reshape(batch, seq, -1), k.reshape(batch, seq, -1),
                               v_t, batch, seq)
            w_out = mla_w_out[j]
        xt = _mixer_out_ffn(xt, o.reshape(t, -1), w_out.astype(bf), g[3], g[4], g[5],
                            ffn_in, ffn_out, i, 1)
    return xt.reshape(batch, seq, D_MODEL)
```

```python
import functools
import math

import jax
import jax.numpy as jnp
from jax import lax
from jax.experimental import pallas as pl
from jax.experimental.pallas import tpu as pltpu

D_MODEL = 1024
EPS = 1e-6
D_FF = 2816
FFN_RESIDUAL_WEIGHT = 0.5

SB_HEADS = 16
SB_HEAD_DIM = 64

MLA_HEADS = 16
MLA_Q_LORA = 256
MLA_KV_LORA = 128
MLA_NOPE_DIM = 64
MLA_ROPE_DIM = 32
MLA_V_DIM = 64
ROPE_THETA = 10000.0

LANES = 128
HEAD_SLAB = 128
HEADS_PER_STEP = 2
VMEM_LIMIT_BYTES = 56 * 1024 * 1024

SB_TAIL_CUTOFF = 104.0
SB_BLOCK = 128
SB_HEADS_PER_STEP = 16

MLA_Q_BLOCK = 1024
MLA_K_BLOCK = 512
MLA_PROJ_TOKENS = 512
MLA_V_CHUNK = 256
MLA_SCORE_SCALE = math.log2(math.e) / math.sqrt(MLA_NOPE_DIM + MLA_ROPE_DIM)
MLA_DENOM_ROWS = 16


def _rms(x, g):
    return x * lax.rsqrt(jnp.mean(x * x, axis=-1, keepdims=True) + EPS) * g


def _dot(a, b):
    return jnp.dot(a, b, preferred_element_type=jnp.float32)


def _dot_nt(a, b):
    return lax.dot_general(a, b, (((1,), (1,)), ((), ())),
                           preferred_element_type=jnp.float32)


def _params(semantics):
    return pltpu.CompilerParams(dimension_semantics=semantics,
                                vmem_limit_bytes=VMEM_LIMIT_BYTES)


def _const_spec(shape):
    return pl.BlockSpec(shape, lambda *_: (0,) * len(shape),
                        pipeline_mode=pl.Buffered(1))


FFN_TOKENS = 512


def _ffn_block(x, gpre_ref, gpost_ref, win_ref, wout_ref):
    xn = _rms(x, gpre_ref[...]).astype(jnp.bfloat16)
    h = _dot(xn, win_ref[...])
    gate = h[:, :D_FF]
    up = h[:, D_FF:]
    act = (gate * jax.nn.sigmoid(gate) * up).astype(jnp.bfloat16)
    f = _dot(act, wout_ref[...])
    return x + FFN_RESIDUAL_WEIGHT * _rms(f, gpost_ref[...])


def _ffn_kernel(x_ref, gpre_ref, gpost_ref, win_ref, wout_ref, y_ref):
    y_ref[...] = _ffn_block(x_ref[...], gpre_ref, gpost_ref, win_ref, wout_ref)


def _mixer_out_ffn_kernel(x_ref, o_ref, wo_ref, gmix_ref, gpre_ref, gpost_ref, win_ref, wout_ref,
                          y_ref):
    x = x_ref[...] + _rms(_dot(o_ref[...], wo_ref[...]), gmix_ref[...])
    y_ref[...] = _ffn_block(x, gpre_ref, gpost_ref, win_ref, wout_ref)


def _ffn_specs(layer, half):
    def weight(rows, cols):
        return pl.BlockSpec((None, None, rows, cols), lambda *_: (layer, half, 0, 0),
                            pipeline_mode=pl.Buffered(1))
    return [
        _const_spec((1, D_MODEL)),
        _const_spec((1, D_MODEL)),
        weight(D_MODEL, 2 * D_FF),
        weight(D_FF, D_MODEL),
    ]


def _ffn(x, g_pre, g_post, w_in, w_out, layer, half):
    t = x.shape[0]
    rows = pl.BlockSpec((FFN_TOKENS, D_MODEL), lambda i: (i, 0))
    return pl.pallas_call(
        _ffn_kernel,
        grid=(t // FFN_TOKENS,),
        in_specs=[rows] + _ffn_specs(layer, half),
        out_specs=rows,
        out_shape=jax.ShapeDtypeStruct((t, D_MODEL), jnp.float32),
        compiler_params=_params(("parallel",)),
        name="ffn",
    )(x, g_pre, g_post, w_in, w_out)


def _mixer_out_ffn(x, o, w_o, g_mix, g_pre, g_post, w_in, w_out, layer, half):
    t = x.shape[0]
    rows = pl.BlockSpec((FFN_TOKENS, D_MODEL), lambda i: (i, 0))
    return pl.pallas_call(
        _mixer_out_ffn_kernel,
        grid=(t // FFN_TOKENS,),
        in_specs=[rows, rows, _const_spec((D_MODEL, D_MODEL)), _const_spec((1, D_MODEL))]
        + _ffn_specs(layer, half),
        out_specs=rows,
        out_shape=jax.ShapeDtypeStruct((t, D_MODEL), jnp.float32),
        compiler_params=_params(("parallel",)),
        name="mixer_out_ffn",
    )(x, o, w_o, g_mix, g_pre, g_post, w_in, w_out)


def _sb_qkv_kernel(x_ref, g_ref, wqk_ref, wvt_ref, qk_ref, vt_ref):
    h = _rms(x_ref[...], g_ref[...]).astype(jnp.bfloat16)
    width = SB_HEADS * SB_HEAD_DIM
    y = _dot(h, wqk_ref[...])
    qk_ref[:, :width] = (y[:, :width] * (1.0 / math.sqrt(SB_HEAD_DIM))).astype(qk_ref.dtype)
    qk_ref[:, width:] = y[:, width:].astype(qk_ref.dtype)
    v_t = _dot_nt(wvt_ref[...], h).astype(vt_ref.dtype)
    for ci in range(vt_ref.shape[0]):
        vt_ref[ci] = v_t[:, ci * SB_BLOCK:(ci + 1) * SB_BLOCK]


def _sb_qkv(x, g, w_qk, w_vt, tm=512):
    t = x.shape[0]
    width = SB_HEADS * SB_HEAD_DIM
    chunks = tm // SB_BLOCK
    return pl.pallas_call(
        _sb_qkv_kernel,
        grid=(t // tm,),
        in_specs=[
            pl.BlockSpec((tm, D_MODEL), lambda i: (i, 0)),
            _const_spec((1, D_MODEL)),
            _const_spec((D_MODEL, 2 * width)),
            _const_spec((width, D_MODEL)),
        ],
        out_specs=[
            pl.BlockSpec((tm, 2 * width), lambda i: (i, 0)),
            pl.BlockSpec((chunks, width, SB_BLOCK), lambda i: (i, 0, 0)),
        ],
        out_shape=[
            jax.ShapeDtypeStruct((t, 2 * width), jnp.bfloat16),
            jax.ShapeDtypeStruct((t // SB_BLOCK, width, SB_BLOCK), jnp.bfloat16),
        ],
        compiler_params=_params(("parallel",)),
        name="sb_qkv",
    )(x, g, w_qk, w_vt)


def _sb_attn_kernel(q_ref, k_ref, v_ref, o_ref):
    qi = pl.program_id(2)
    blk = SB_BLOCK
    heads = range(SB_HEADS_PER_STEP)
    lane = lax.broadcasted_iota(jnp.int32, (blk, LANES), 1)
    key = lax.broadcasted_iota(jnp.int32, (blk, blk), 0)
    qry = lax.broadcasted_iota(jnp.int32, (blk, blk), 1)
    causal = key < qry
    half = blk // 2

    def cum_matrix(tk):
        ur = lax.broadcasted_iota(jnp.int32, (tk, 2 * tk), 0)
        uc = lax.broadcasted_iota(jnp.int32, (tk, 2 * tk), 1)
        return jnp.where((uc % tk) > ur, 1.0, 0.0).astype(jnp.bfloat16)

    cum = {blk: cum_matrix(blk), half: cum_matrix(half)}
    slabs = [slice((h // 2) * LANES, (h // 2 + 1) * LANES) for h in heads]
    q_heads = []
    for h in heads:
        q = q_ref[0, :, slabs[h]]
        q_heads.append(jnp.where((lane // SB_HEAD_DIM) == h % 2, q, jnp.zeros_like(q)))

    def logits(start, tk, h):
        start = pl.multiple_of(start, tk)
        return _dot_nt(k_ref[0, pl.ds(start, tk), slabs[h]], q_heads[h])

    def softplus_parts(z, mask):
        sp = jnp.maximum(z, 0.0) + jnp.log(1.0 + jnp.exp2(jnp.abs(z) * (-math.log2(math.e))))
        sp_m = sp if mask is None else jnp.where(mask, sp, 0.0)
        hi = sp_m.astype(jnp.bfloat16)
        lo = (sp_m - hi.astype(jnp.float32)).astype(jnp.bfloat16)
        return sp, jnp.concatenate([hi, lo], axis=0), jnp.sum(sp_m, axis=0, keepdims=True)

    def weights(z, sp, inner, tail, mask):
        w = jnp.exp(z - sp - inner - tail)
        return (w if mask is None else jnp.where(mask, w, 0.0)).astype(jnp.bfloat16)

    def values(v_t, a, valid=None):
        if valid is not None:
            v_t = jnp.where(valid, v_t, jnp.zeros_like(v_t))
        return _dot(v_t, a)

    def head_rows(h):
        return slice(h * SB_HEAD_DIM, (h + 1) * SB_HEAD_DIM)

    b1 = jnp.maximum(qi - 1, 0)
    b2 = jnp.maximum(qi - 2, 0)
    spans = [(qi * blk, blk, qi, slice(0, blk), causal, None),
             (b1 * blk, blk, b1, slice(0, blk), None, qi >= 1),
             (b2 * blk + half, half, b2, slice(half, blk), None, qi >= 2)]
    tiles = [(d, h) for d in range(len(spans)) for h in heads]
    z = {(d, h): logits(spans[d][0], spans[d][1], h) for d, h in tiles}
    parts = {t: softplus_parts(z[t], spans[t[0]][4]) for t in tiles}
    inner = {t: _dot(cum[spans[t[0]][1]], parts[t][1]) for t in tiles}
    tails = [jnp.zeros((1, blk), jnp.float32) for _ in heads]
    a = {}
    for d, h in tiles:
        valid = spans[d][5]
        a[d, h] = weights(z[d, h], parts[d, h][0], inner[d, h], tails[h], spans[d][4])
        total = parts[d, h][2]
        tails[h] = tails[h] + (total if valid is None else jnp.where(valid, total, 0.0))
    accs = [sum(values(v_ref[0, chunk, head_rows(h), lanes], a[d, h], valid)
                for d, (_, _, chunk, lanes, _, valid) in enumerate(spans)) for h in heads]

    def live(state):
        smallest = functools.reduce(jnp.minimum, [state[1 + 2 * h] for h in heads])
        return jnp.logical_and(state[0] >= 0, jnp.min(smallest) < SB_TAIL_CUTOFF)

    def older(state):
        hb = state[0]
        newer_half = hb % 2 == 1
        out = [hb - 1]
        for h in heads:
            tail, acc = state[1 + 2 * h], state[2 + 2 * h]
            zz = logits(hb * half, half, h)
            sp, hilo, total = softplus_parts(zz, None)
            v_blk = v_ref[0, hb // 2, head_rows(h), :]
            v_t = jnp.where(newer_half, v_blk[:, half:], v_blk[:, :half])
            acc = acc + values(v_t, weights(zz, sp, _dot(cum[half], hilo), tail, None))
            out += [tail + total, acc]
        return tuple(out)

    init = [2 * qi - 4]
    for h in heads:
        init += [tails[h], accs[h]]
    state = lax.while_loop(live, older, tuple(init))
    o_t = jnp.concatenate([state[2 + 2 * h] for h in heads], axis=0)
    o_ref[0] = o_t.T.astype(o_ref.dtype)


def _sb_attention(qk, v_t, batch, seq):
    groups = SB_HEADS // SB_HEADS_PER_STEP
    width = SB_HEADS_PER_STEP * SB_HEAD_DIM
    nq = seq // SB_BLOCK
    return pl.pallas_call(
        _sb_attn_kernel,
        grid=(batch, groups, nq),
        in_specs=[
            pl.BlockSpec((1, SB_BLOCK, width), lambda b, p, i: (b, i, p)),
            pl.BlockSpec((1, seq, width), lambda b, p, i: (b, 0, groups + p),
                         pipeline_mode=pl.Buffered(1)),
            pl.BlockSpec((1, nq, width, SB_BLOCK), lambda b, p, i: (b, 0, p, 0),
                         pipeline_mode=pl.Buffered(1)),
        ],
        out_specs=pl.BlockSpec((1, SB_BLOCK, width), lambda b, p, i: (b, i, p)),
        out_shape=jax.ShapeDtypeStruct((batch, seq, SB_HEADS * SB_HEAD_DIM), jnp.bfloat16),
        compiler_params=_params(("parallel", "parallel", "arbitrary")),
        name="sb_attn",
    )(qk, qk, v_t)


def _mla_proj_kernel(x_ref, pos_ref, freq_ref, g_ref, win_ref, qn_ref, kvn_ref,
                     wuq_ref, wuk_ref, wuv_ref, q_ref, k_ref, v_ref):
    h = _rms(x_ref[...], g_ref[...]).astype(jnp.bfloat16)
    proj = _dot(h, win_ref[...])
    c_q = proj[:, :MLA_Q_LORA]
    c_kv = proj[:, MLA_Q_LORA:MLA_Q_LORA + MLA_KV_LORA]
    kr = proj[:, MLA_Q_LORA + MLA_KV_LORA:MLA_Q_LORA + MLA_KV_LORA + HEAD_SLAB]
    kr_rot = proj[:, MLA_Q_LORA + MLA_KV_LORA + HEAD_SLAB:]

    tm = x_ref.shape[0]
    lane = lax.broadcasted_iota(jnp.int32, (tm, HEAD_SLAB), 1)
    ang = pos_ref[...].astype(jnp.float32) * freq_ref[...]
    is_rope = jnp.logical_and(lane >= MLA_NOPE_DIM, lane < MLA_NOPE_DIM + MLA_ROPE_DIM)
    cos = jnp.where(lane < MLA_NOPE_DIM, 1.0, jnp.where(is_rope, jnp.cos(ang), 0.0))
    sin = jnp.where(is_rope, jnp.sin(ang), 0.0)

    qn = _rms(c_q, qn_ref[...]).astype(jnp.bfloat16)
    q_all = _dot(qn, wuq_ref[...])
    kvn = _rms(c_kv, kvn_ref[...]).astype(jnp.bfloat16)
    k_nope = _dot(kvn, wuk_ref[...])
    v_t = _dot_nt(wuv_ref[...], kvn).astype(v_ref.dtype)
    for ci in range(v_ref.shape[0]):
        v_ref[ci] = v_t[:, ci * MLA_V_CHUNK:(ci + 1) * MLA_V_CHUNK]
    k_rope = kr * cos + kr_rot * sin
    cos_q = cos * MLA_SCORE_SCALE
    sin_q = sin * MLA_SCORE_SCALE
    width = MLA_HEADS * HEAD_SLAB
    for hd in range(MLA_HEADS):
        sl = slice(hd * HEAD_SLAB, (hd + 1) * HEAD_SLAB)
        rot = slice(width + hd * HEAD_SLAB, width + (hd + 1) * HEAD_SLAB)
        q_ref[:, sl] = (q_all[:, sl] * cos_q + q_all[:, rot] * sin_q).astype(q_ref.dtype)
        k_ref[:, sl] = (k_nope[:, sl] + k_rope).astype(k_ref.dtype)


def _mla_weights(w_in, w_uq, w_ukv):
    half = MLA_ROPE_DIM // 2
    pad = HEAD_SLAB - MLA_NOPE_DIM - MLA_ROPE_DIM

    def slab(nope, rope):
        return jnp.concatenate(
            [nope, rope, jnp.zeros(rope.shape[:-1] + (pad,), rope.dtype)], axis=-1)

    def rotate(rope):
        return jnp.concatenate([-rope[..., half:], rope[..., :half]], axis=-1)

    lat = MLA_Q_LORA + MLA_KV_LORA
    kr_w = w_in[:, lat:]
    zeros_in = jnp.zeros((D_MODEL, MLA_NOPE_DIM), w_in.dtype)
    w_in_p = jnp.concatenate(
        [w_in[:, :lat], slab(zeros_in, kr_w), slab(zeros_in, rotate(kr_w))], axis=1)

    uq = w_uq.reshape(MLA_Q_LORA, MLA_HEADS, MLA_NOPE_DIM + MLA_ROPE_DIM)
    uq_nope, uq_rope = uq[..., :MLA_NOPE_DIM], uq[..., MLA_NOPE_DIM:]
    w_uq_p = jnp.concatenate(
        [slab(uq_nope, uq_rope).reshape(MLA_Q_LORA, -1),
         slab(jnp.zeros_like(uq_nope), rotate(uq_rope)).reshape(MLA_Q_LORA, -1)], axis=1)

    ukv = w_ukv.reshape(MLA_KV_LORA, MLA_HEADS, MLA_NOPE_DIM + MLA_V_DIM)
    uk = ukv[..., :MLA_NOPE_DIM]
    w_uk_p = jnp.concatenate([uk, jnp.zeros_like(uk)], axis=-1).reshape(MLA_KV_LORA, -1)
    w_uv = ukv[..., MLA_NOPE_DIM:].reshape(MLA_KV_LORA, -1).T
    bf = jnp.bfloat16
    return w_in_p.astype(bf), w_uq_p.astype(bf), w_uk_p.astype(bf), w_uv.astype(bf)


def _rope_freq_slab():
    inv_freq = ROPE_THETA ** (-jnp.arange(0, MLA_ROPE_DIM, 2, dtype=jnp.float32) / MLA_ROPE_DIM)
    zeros = jnp.zeros((MLA_NOPE_DIM,), jnp.float32)
    pad = jnp.zeros((HEAD_SLAB - MLA_NOPE_DIM - MLA_ROPE_DIM,), jnp.float32)
    return jnp.concatenate([zeros, inv_freq, inv_freq, pad])[None, :]


def _mla_proj(x, pos, g, w_in_p, q_norm, kv_norm, w_uq_p, w_uk_p, w_uv):
    tm = MLA_PROJ_TOKENS
    chunks = tm // MLA_V_CHUNK
    t = x.shape[0]
    width = MLA_HEADS * HEAD_SLAB
    vwidth = MLA_HEADS * MLA_V_DIM
    row = lambda i: (i, 0)
    return pl.pallas_call(
        _mla_proj_kernel,
        grid=(t // tm,),
        in_specs=[
            pl.BlockSpec((tm, D_MODEL), row),
            pl.BlockSpec((tm, 1), row),
            _const_spec((1, HEAD_SLAB)),
            _const_spec((1, D_MODEL)),
            _const_spec(w_in_p.shape),
            _const_spec((1, MLA_Q_LORA)),
            _const_spec((1, MLA_KV_LORA)),
            _const_spec(w_uq_p.shape),
            _const_spec(w_uk_p.shape),
            _const_spec(w_uv.shape),
        ],
        out_specs=[
            pl.BlockSpec((tm, width), row),
            pl.BlockSpec((tm, width), row),
            pl.BlockSpec((chunks, vwidth, MLA_V_CHUNK), lambda i: (i, 0, 0)),
        ],
        out_shape=[
            jax.ShapeDtypeStruct((t, width), jnp.bfloat16),
            jax.ShapeDtypeStruct((t, width), jnp.bfloat16),
            jax.ShapeDtypeStruct((t // MLA_V_CHUNK, vwidth, MLA_V_CHUNK), jnp.bfloat16),
        ],
        compiler_params=_params(("parallel",)),
        name="mla_proj",
    )(x, pos, _rope_freq_slab(), g, w_in_p, q_norm, kv_norm, w_uq_p, w_uk_p, w_uv)


def _mla_attn_kernel(q_ref, k_ref, v_ref, o_ref,
                     s0_ref, s1_ref, p0_ref, p1_ref, mx_ref, m_ref, alpha_ref, acc_ref):
    qi = pl.program_id(2)
    tq, tk = MLA_Q_BLOCK, MLA_K_BLOCK
    chunks = tk // MLA_V_CHUNK
    heads = range(HEADS_PER_STEP)
    q_heads = [q_ref[0, :, h * HEAD_SLAB:(h + 1) * HEAD_SLAB] for h in heads]
    ones = jnp.ones((MLA_DENOM_ROWS, tk), jnp.bfloat16)
    ratio = tq // tk
    n_full = qi * ratio
    s_bufs = (s0_ref, s1_ref)
    p_bufs = (p0_ref, p1_ref)

    def scores(j, parity, first=0):
        start = pl.multiple_of(j * tk, tk)
        for h in heads:
            k_blk = k_ref[0, pl.ds(start, tk), h * HEAD_SLAB:(h + 1) * HEAD_SLAB]
            s = _dot_nt(k_blk, q_heads[h][first:])
            s_bufs[parity][h, :, first:tq] = s
            mx_ref[parity, h, :, first:] = jnp.max(s, axis=0, keepdims=True)

    def values(j, parity, first=0):
        for h in heads:
            v_t = jnp.concatenate(
                [v_ref[0, j * chunks + ci, h * MLA_V_DIM:(h + 1) * MLA_V_DIM, :]
                 for ci in range(chunks)], axis=1)
            pv = _dot(jnp.concatenate([v_t, ones], axis=0), p_bufs[parity][h, :, first:tq])
            acc_ref[h, :, first:] = alpha_ref[parity, h, :, first:] * acc_ref[h, :, first:] + pv

    def softmax(parity, mask, first=0, last=tq):
        cols = slice(first, last)
        for h in heads:
            s = s_bufs[parity][h, :, cols]
            if mask is None:
                block_max = mx_ref[parity, h, :, cols]
            else:
                s = jnp.where(mask, s, -jnp.inf)
                block_max = jnp.max(s, axis=0, keepdims=True)
            m_old = m_ref[h, :, cols]
            m_new = jnp.maximum(m_old, block_max)
            p_bufs[parity][h, :, cols] = jnp.exp2(s - m_new).astype(jnp.bfloat16)
            alpha_ref[parity, h, :, cols] = jnp.exp2(m_old - m_new)
            m_ref[h, :, cols] = m_new

    def step(i, parity):
        softmax(parity, None)
        scores(i + 1, 1 - parity)
        values(jnp.maximum(i - 1, 0), 1 - parity)

    def diagonal_steps():
        key = lax.broadcasted_iota(jnp.int32, (tk, tk), 0)
        qry = lax.broadcasted_iota(jnp.int32, (tk, tk), 1)
        for off in range(ratio):
            if off + 1 < ratio:
                scores(n_full + off + 1, (off + 1) % 2, (off + 1) * tk)
            softmax(off % 2, key <= qry, off * tk, (off + 1) * tk)
            if off + 1 < ratio:
                softmax(off % 2, None, (off + 1) * tk)
            values(jnp.maximum(n_full + off - 1, 0), (off + 1) % 2, max(off - 1, 0) * tk)
        values(n_full + ratio - 1, (ratio - 1) % 2, (ratio - 1) * tk)
        o_t = jnp.concatenate(
            [acc_ref[h, :MLA_V_DIM] / acc_ref[h, MLA_V_DIM:MLA_V_DIM + 1] for h in heads], axis=0)
        o_ref[0] = o_t.T.astype(o_ref.dtype)

    m_ref[...] = jnp.full(m_ref.shape, -jnp.inf, jnp.float32)
    alpha_ref[...] = jnp.ones(alpha_ref.shape, jnp.float32)
    acc_ref[...] = jnp.zeros(acc_ref.shape, jnp.float32)

    p1_ref[...] = jnp.zeros(p1_ref.shape, jnp.bfloat16)
    scores(0, 0)

    def two_steps(ii, carry):
        step(2 * ii, 0)
        step(2 * ii + 1, 1)
        return carry

    lax.fori_loop(0, n_full // 2, two_steps, 0)
    diagonal_steps()


def _mla_attention(q, k, v, batch, seq):
    assert MLA_Q_BLOCK % (2 * MLA_K_BLOCK) == 0
    assert seq % MLA_Q_BLOCK == 0 and MLA_K_BLOCK % MLA_V_CHUNK == 0
    pairs = MLA_HEADS // HEADS_PER_STEP
    nq = seq // MLA_Q_BLOCK
    qk_w = HEADS_PER_STEP * HEAD_SLAB
    return pl.pallas_call(
        _mla_attn_kernel,
        grid=(batch, pairs, nq),
        in_specs=[
            pl.BlockSpec((1, MLA_Q_BLOCK, qk_w), lambda b, p, i: (b, i, p)),
            pl.BlockSpec((1, seq, qk_w), lambda b, p, i: (b, 0, p)),
            pl.BlockSpec((1, seq // MLA_V_CHUNK, HEADS_PER_STEP * MLA_V_DIM, MLA_V_CHUNK),
                         lambda b, p, i: (b, 0, p, 0)),
        ],
        out_specs=pl.BlockSpec((1, MLA_Q_BLOCK, LANES), lambda b, p, i: (b, i, p)),
        out_shape=jax.ShapeDtypeStruct((batch, seq, MLA_HEADS * MLA_V_DIM), jnp.bfloat16),
        scratch_shapes=[
            pltpu.VMEM((HEADS_PER_STEP, MLA_K_BLOCK, MLA_Q_BLOCK), jnp.float32),
            pltpu.VMEM((HEADS_PER_STEP, MLA_K_BLOCK, MLA_Q_BLOCK), jnp.float32),
            pltpu.VMEM((HEADS_PER_STEP, MLA_K_BLOCK, MLA_Q_BLOCK), jnp.bfloat16),
            pltpu.VMEM((HEADS_PER_STEP, MLA_K_BLOCK, MLA_Q_BLOCK), jnp.bfloat16),
            pltpu.VMEM((2, HEADS_PER_STEP, 1, MLA_Q_BLOCK), jnp.float32),
            pltpu.VMEM((HEADS_PER_STEP, 1, MLA_Q_BLOCK), jnp.float32),
            pltpu.VMEM((2, HEADS_PER_STEP, 1, MLA_Q_BLOCK), jnp.float32),
            pltpu.VMEM((HEADS_PER_STEP, MLA_V_DIM + MLA_DENOM_ROWS, MLA_Q_BLOCK), jnp.float32),
        ],
        compiler_params=_params(("parallel", "parallel", "arbitrary")),
        name="mla_attn",
    )(q, k, v)


def kernel(x, positions, norm_g, ffn_w_in, ffn_w_out, sb_w_in, sb_w_out, mla_w_in,
           mla_q_norm, mla_w_uq, mla_kv_norm, mla_w_ukv, mla_w_out):
    batch, seq, _ = x.shape
    t = batch * seq
    bf = jnp.bfloat16
    depth = norm_g.shape[0]
    xt = x.reshape(t, D_MODEL)
    pos = positions.reshape(t, 1)
    ffn_in = ffn_w_in.astype(bf)
    ffn_out = ffn_w_out.astype(bf)
    for i in range(depth):
        g = norm_g[i][:, None, :]
        xt = _ffn(xt, g[0], g[1], ffn_in, ffn_out, i, 0)
        j = i // 2
        if i % 2 == 0:
            width = SB_HEADS * SB_HEAD_DIM
            w_qk = sb_w_in[j][:, :2 * width].astype(bf)
            w_vt = sb_w_in[j][:, 2 * width:].T.astype(bf)
            qk, v_t = _sb_qkv(xt, g[2], w_qk, w_vt)
            o = _sb_attention(qk.reshape(batch, seq, -1),
                              v_t.reshape(batch, seq // SB_BLOCK, width, SB_BLOCK), batch, seq)
            w_out = sb_w_out[j]
        else:
            w_in_p, w_uq_p, w_uk_p, w_uv = _mla_weights(mla_w_in[j], mla_w_uq[j], mla_w_ukv[j])
            q, k, v = _mla_proj(xt, pos, g[2], w_in_p, mla_q_norm[j][None, :],
                                mla_kv_norm[j][None, :], w_uq_p, w_uk_p, w_uv)
            v_t = v.reshape(batch, seq // MLA_V_CHUNK, MLA_HEADS * MLA_V_DIM, MLA_V_CHUNK)
            o = _mla_attention(q.reshape(batch, seq, -1), k.reshape(batch, seq, -1),
                               v_t, batch, seq)
            w_out = mla_w_out[j]
        xt = _mixer_out_ffn(xt, o.reshape(t, -1), w_out.astype(bf), g[3], g[4], g[5],
                            ffn_in, ffn_out, i, 1)
    return xt.reshape(batch, seq, D_MODEL)
```

```python
import functools
import math

import jax
import jax.numpy as jnp
from jax import lax
from jax.experimental import pallas as pl
from jax.experimental.pallas import tpu as pltpu

D_MODEL = 1024
EPS = 1e-6
D_FF = 2816
FFN_RESIDUAL_WEIGHT = 0.5

SB_HEADS = 16
SB_HEAD_DIM = 64

MLA_HEADS = 16
MLA_Q_LORA = 256
MLA_KV_LORA = 128
MLA_NOPE_DIM = 64
MLA_ROPE_DIM = 32
MLA_V_DIM = 64
ROPE_THETA = 10000.0

LANES = 128
HEAD_SLAB = 128
HEADS_PER_STEP = 4
VMEM_LIMIT_BYTES = 56 * 1024 * 1024

SB_TAIL_CUTOFF = 104.0
SB_BLOCK = 128
SB_HEADS_PER_STEP = 16

MLA_Q_BLOCK = 1024
MLA_K_BLOCK = 512
MLA_PROJ_TOKENS = 512
MLA_V_CHUNK = 256
MLA_SCORE_SCALE = math.log2(math.e) / math.sqrt(MLA_NOPE_DIM + MLA_ROPE_DIM)
MLA_DENOM_ROWS = 16


def _rms(x, g):
    return x * lax.rsqrt(jnp.mean(x * x, axis=-1, keepdims=True) + EPS) * g


def _dot(a, b):
    return jnp.dot(a, b, preferred_element_type=jnp.float32)


def _dot_nt(a, b):
    return lax.dot_general(a, b, (((1,), (1,)), ((), ())),
                           preferred_element_type=jnp.float32)


def _params(semantics):
    return pltpu.CompilerParams(dimension_semantics=semantics,
                                vmem_limit_bytes=VMEM_LIMIT_BYTES)


def _const_spec(shape):
    return pl.BlockSpec(shape, lambda *_: (0,) * len(shape),
                        pipeline_mode=pl.Buffered(1))


FFN_TOKENS = 512


def _ffn_block(x, gpre_ref, gpost_ref, win_ref, wout_ref):
    xn = _rms(x, gpre_ref[...]).astype(jnp.bfloat16)
    h = _dot(xn, win_ref[...])
    gate = h[:, :D_FF]
    up = h[:, D_FF:]
    act = (gate * jax.nn.sigmoid(gate) * up).astype(jnp.bfloat16)
    f = _dot(act, wout_ref[...])
    return x + FFN_RESIDUAL_WEIGHT * _rms(f, gpost_ref[...])


def _ffn_kernel(x_ref, gpre_ref, gpost_ref, win_ref, wout_ref, y_ref):
    y_ref[...] = _ffn_block(x_ref[...], gpre_ref, gpost_ref, win_ref, wout_ref)


def _mixer_out_ffn_kernel(x_ref, o_ref, wo_ref, gmix_ref, gpre_ref, gpost_ref, win_ref, wout_ref,
                          y_ref):
    x = x_ref[...] + _rms(_dot(o_ref[...], wo_ref[...]), gmix_ref[...])
    y_ref[...] = _ffn_block(x, gpre_ref, gpost_ref, win_ref, wout_ref)


def _ffn_specs(layer, half):
    def weight(rows, cols):
        return pl.BlockSpec((None, None, rows, cols), lambda *_: (layer, half, 0, 0),
                            pipeline_mode=pl.Buffered(1))
    return [
        _const_spec((1, D_MODEL)),
        _const_spec((1, D_MODEL)),
        weight(D_MODEL, 2 * D_FF),
        weight(D_FF, D_MODEL),
    ]


def _ffn(x, g_pre, g_post, w_in, w_out, layer, half):
    t = x.shape[0]
    rows = pl.BlockSpec((FFN_TOKENS, D_MODEL), lambda i: (i, 0))
    return pl.pallas_call(
        _ffn_kernel,
        grid=(t // FFN_TOKENS,),
        in_specs=[rows] + _ffn_specs(layer, half),
        out_specs=rows,
        out_shape=jax.ShapeDtypeStruct((t, D_MODEL), jnp.float32),
        compiler_params=_params(("parallel",)),
        name="ffn",
    )(x, g_pre, g_post, w_in, w_out)


def _mixer_out_ffn(x, o, w_o, g_mix, g_pre, g_post, w_in, w_out, layer, half):
    t = x.shape[0]
    rows = pl.BlockSpec((FFN_TOKENS, D_MODEL), lambda i: (i, 0))
    return pl.pallas_call(
        _mixer_out_ffn_kernel,
        grid=(t // FFN_TOKENS,),
        in_specs=[rows, rows, _const_spec((D_MODEL, D_MODEL)), _const_spec((1, D_MODEL))]
        + _ffn_specs(layer, half),
        out_specs=rows,
        out_shape=jax.ShapeDtypeStruct((t, D_MODEL), jnp.float32),
        compiler_params=_params(("parallel",)),
        name="mixer_out_ffn",
    )(x, o, w_o, g_mix, g_pre, g_post, w_in, w_out)


def _sb_qkv_kernel(x_ref, g_ref, wqk_ref, wvt_ref, qk_ref, vt_ref):
    h = _rms(x_ref[...], g_ref[...]).astype(jnp.bfloat16)
    width = SB_HEADS * SB_HEAD_DIM
    y = _dot(h, wqk_ref[...])
    qk_ref[:, :width] = (y[:, :width] * (1.0 / math.sqrt(SB_HEAD_DIM))).astype(qk_ref.dtype)
    qk_ref[:, width:] = y[:, width:].astype(qk_ref.dtype)
    v_t = _dot_nt(wvt_ref[...], h).astype(vt_ref.dtype)
    for ci in range(vt_ref.shape[0]):
        vt_ref[ci] = v_t[:, ci * SB_BLOCK:(ci + 1) * SB_BLOCK]


def _sb_qkv(x, g, w_qk, w_vt, tm=512):
    t = x.shape[0]
    width = SB_HEADS * SB_HEAD_DIM
    chunks = tm // SB_BLOCK
    return pl.pallas_call(
        _sb_qkv_kernel,
        grid=(t // tm,),
        in_specs=[
            pl.BlockSpec((tm, D_MODEL), lambda i: (i, 0)),
            _const_spec((1, D_MODEL)),
            _const_spec((D_MODEL, 2 * width)),
            _const_spec((width, D_MODEL)),
        ],
        out_specs=[
            pl.BlockSpec((tm, 2 * width), lambda i: (i, 0)),
            pl.BlockSpec((chunks, width, SB_BLOCK), lambda i: (i, 0, 0)),
        ],
        out_shape=[
            jax.ShapeDtypeStruct((t, 2 * width), jnp.bfloat16),
            jax.ShapeDtypeStruct((t // SB_BLOCK, width, SB_BLOCK), jnp.bfloat16),
        ],
        compiler_params=_params(("parallel",)),
        name="sb_qkv",
    )(x, g, w_qk, w_vt)


def _sb_attn_kernel(q_ref, k_ref, v_ref, o_ref):
    qi = pl.program_id(2)
    blk = SB_BLOCK
    heads = range(SB_HEADS_PER_STEP)
    lane = lax.broadcasted_iota(jnp.int32, (blk, LANES), 1)
    key = lax.broadcasted_iota(jnp.int32, (blk, blk), 0)
    qry = lax.broadcasted_iota(jnp.int32, (blk, blk), 1)
    causal = key < qry
    half = blk // 2

    def cum_matrix(tk):
        ur = lax.broadcasted_iota(jnp.int32, (tk, 2 * tk), 0)
        uc = lax.broadcasted_iota(jnp.int32, (tk, 2 * tk), 1)
        return jnp.where((uc % tk) > ur, 1.0, 0.0).astype(jnp.bfloat16)

    cum = {blk: cum_matrix(blk), half: cum_matrix(half)}
    slabs = [slice((h // 2) * LANES, (h // 2 + 1) * LANES) for h in heads]
    q_heads = []
    for h in heads:
        q = q_ref[0, :, slabs[h]]
        q_heads.append(jnp.where((lane // SB_HEAD_DIM) == h % 2, q, jnp.zeros_like(q)))

    def logits(start, tk, h):
        start = pl.multiple_of(start, tk)
        return _dot_nt(k_ref[0, pl.ds(start, tk), slabs[h]], q_heads[h])

    def softplus_parts(z, mask):
        sp = jnp.maximum(z, 0.0) + jnp.log(1.0 + jnp.exp2(jnp.abs(z) * (-math.log2(math.e))))
        sp_m = sp if mask is None else jnp.where(mask, sp, 0.0)
        hi = sp_m.astype(jnp.bfloat16)
        lo = (sp_m - hi.astype(jnp.float32)).astype(jnp.bfloat16)
        return sp, jnp.concatenate([hi, lo], axis=0), jnp.sum(sp_m, axis=0, keepdims=True)

    def weights(z, sp, inner, tail, mask):
        w = jnp.exp(z - sp - inner - tail)
        return (w if mask is None else jnp.where(mask, w, 0.0)).astype(jnp.bfloat16)

    def values(v_t, a, valid=None):
        if valid is not None:
            v_t = jnp.where(valid, v_t, jnp.zeros_like(v_t))
        return _dot(v_t, a)

    def head_rows(h):
        return slice(h * SB_HEAD_DIM, (h + 1) * SB_HEAD_DIM)

    b1 = jnp.maximum(qi - 1, 0)
    b2 = jnp.maximum(qi - 2, 0)
    spans = [(qi * blk, blk, qi, slice(0, blk), causal, None),
             (b1 * blk, blk, b1, slice(0, blk), None, qi >= 1),
             (b2 * blk + half, half, b2, slice(half, blk), None, qi >= 2)]
    tiles = [(d, h) for d in range(len(spans)) for h in heads]
    z = {(d, h): logits(spans[d][0], spans[d][1], h) for d, h in tiles}
    parts = {t: softplus_parts(z[t], spans[t[0]][4]) for t in tiles}
    inner = {t: _dot(cum[spans[t[0]][1]], parts[t][1]) for t in tiles}
    tails = [jnp.zeros((1, blk), jnp.float32) for _ in heads]
    a = {}
    for d, h in tiles:
        valid = spans[d][5]
        a[d, h] = weights(z[d, h], parts[d, h][0], inner[d, h], tails[h], spans[d][4])
        total = parts[d, h][2]
        tails[h] = tails[h] + (total if valid is None else jnp.where(valid, total, 0.0))
    accs = [sum(values(v_ref[0, chunk, head_rows(h), lanes], a[d, h], valid)
                for d, (_, _, chunk, lanes, _, valid) in enumerate(spans)) for h in heads]

    def live(state):
        smallest = functools.reduce(jnp.minimum, [state[1 + 2 * h] for h in heads])
        return jnp.logical_and(state[0] >= 0, jnp.min(smallest) < SB_TAIL_CUTOFF)

    def older(state):
        hb = state[0]
        newer_half = hb % 2 == 1
        out = [hb - 1]
        for h in heads:
            tail, acc = state[1 + 2 * h], state[2 + 2 * h]
            zz = logits(hb * half, half, h)
            sp, hilo, total = softplus_parts(zz, None)
            v_blk = v_ref[0, hb // 2, head_rows(h), :]
            v_t = jnp.where(newer_half, v_blk[:, half:], v_blk[:, :half])
            acc = acc + values(v_t, weights(zz, sp, _dot(cum[half], hilo), tail, None))
            out += [tail + total, acc]
        return tuple(out)

    init = [2 * qi - 4]
    for h in heads:
        init += [tails[h], accs[h]]
    state = lax.while_loop(live, older, tuple(init))
    o_t = jnp.concatenate([state[2 + 2 * h] for h in heads], axis=0)
    o_ref[0] = o_t.T.astype(o_ref.dtype)


def _sb_attention(qk, v_t, batch, seq):
    groups = SB_HEADS // SB_HEADS_PER_STEP
    width = SB_HEADS_PER_STEP * SB_HEAD_DIM
    nq = seq // SB_BLOCK
    return pl.pallas_call(
        _sb_attn_kernel,
        grid=(batch, groups, nq),
        in_specs=[
            pl.BlockSpec((1, SB_BLOCK, width), lambda b, p, i: (b, i, p)),
            pl.BlockSpec((1, seq, width), lambda b, p, i: (b, 0, groups + p),
                         pipeline_mode=pl.Buffered(1)),
            pl.BlockSpec((1, nq, width, SB_BLOCK), lambda b, p, i: (b, 0, p, 0),
                         pipeline_mode=pl.Buffered(1)),
        ],
        out_specs=pl.BlockSpec((1, SB_BLOCK, width), lambda b, p, i: (b, i, p)),
        out_shape=jax.ShapeDtypeStruct((batch, seq, SB_HEADS * SB_HEAD_DIM), jnp.bfloat16),
        compiler_params=_params(("parallel", "parallel", "arbitrary")),
        name="sb_attn",
    )(qk, qk, v_t)


def _mla_proj_kernel(x_ref, pos_ref, freq_ref, g_ref, win_ref, qn_ref, kvn_ref,
                     wuq_ref, wuk_ref, wuv_ref, q_ref, k_ref, v_ref):
    h = _rms(x_ref[...], g_ref[...]).astype(jnp.bfloat16)
    proj = _dot(h, win_ref[...])
    c_q = proj[:, :MLA_Q_LORA]
    c_kv = proj[:, MLA_Q_LORA:MLA_Q_LORA + MLA_KV_LORA]
    kr = proj[:, MLA_Q_LORA + MLA_KV_LORA:MLA_Q_LORA + MLA_KV_LORA + HEAD_SLAB]
    kr_rot = proj[:, MLA_Q_LORA + MLA_KV_LORA + HEAD_SLAB:]

    tm = x_ref.shape[0]
    lane = lax.broadcasted_iota(jnp.int32, (tm, HEAD_SLAB), 1)
    ang = pos_ref[...].astype(jnp.float32) * freq_ref[...]
    is_rope = jnp.logical_and(lane >= MLA_NOPE_DIM, lane < MLA_NOPE_DIM + MLA_ROPE_DIM)
    cos = jnp.where(lane < MLA_NOPE_DIM, 1.0, jnp.where(is_rope, jnp.cos(ang), 0.0))
    sin = jnp.where(is_rope, jnp.sin(ang), 0.0)

    qn = _rms(c_q, qn_ref[...]).astype(jnp.bfloat16)
    q_all = _dot(qn, wuq_ref[...])
    kvn = _rms(c_kv, kvn_ref[...]).astype(jnp.bfloat16)
    k_nope = _dot(kvn, wuk_ref[...])
    v_t = _dot_nt(wuv_ref[...], kvn).astype(v_ref.dtype)
    for ci in range(v_ref.shape[0]):
        v_ref[ci] = v_t[:, ci * MLA_V_CHUNK:(ci + 1) * MLA_V_CHUNK]
    k_rope = kr * cos + kr_rot * sin
    cos_q = cos * MLA_SCORE_SCALE
    sin_q = sin * MLA_SCORE_SCALE
    width = MLA_HEADS * HEAD_SLAB
    for hd in range(MLA_HEADS):
        sl = slice(hd * HEAD_SLAB, (hd + 1) * HEAD_SLAB)
        rot = slice(width + hd * HEAD_SLAB, width + (hd + 1) * HEAD_SLAB)
        q_ref[:, sl] = (q_all[:, sl] * cos_q + q_all[:, rot] * sin_q).astype(q_ref.dtype)
        k_ref[:, sl] = (k_nope[:, sl] + k_rope).astype(k_ref.dtype)


def _mla_weights(w_in, w_uq, w_ukv):
    half = MLA_ROPE_DIM // 2
    pad = HEAD_SLAB - MLA_NOPE_DIM - MLA_ROPE_DIM

    def slab(nope, rope):
        return jnp.concatenate(
            [nope, rope, jnp.zeros(rope.shape[:-1] + (pad,), rope.dtype)], axis=-1)

    def rotate(rope):
        return jnp.concatenate([-rope[..., half:], rope[..., :half]], axis=-1)

    lat = MLA_Q_LORA + MLA_KV_LORA
    kr_w = w_in[:, lat:]
    zeros_in = jnp.zeros((D_MODEL, MLA_NOPE_DIM), w_in.dtype)
    w_in_p = jnp.concatenate(
        [w_in[:, :lat], slab(zeros_in, kr_w), slab(zeros_in, rotate(kr_w))], axis=1)

    uq = w_uq.reshape(MLA_Q_LORA, MLA_HEADS, MLA_NOPE_DIM + MLA_ROPE_DIM)
    uq_nope, uq_rope = uq[..., :MLA_NOPE_DIM], uq[..., MLA_NOPE_DIM:]
    w_uq_p = jnp.concatenate(
        [slab(uq_nope, uq_rope).reshape(MLA_Q_LORA, -1),
         slab(jnp.zeros_like(uq_nope), rotate(uq_rope)).reshape(MLA_Q_LORA, -1)], axis=1)

    ukv = w_ukv.reshape(MLA_KV_LORA, MLA_HEADS, MLA_NOPE_DIM + MLA_V_DIM)
    uk = ukv[..., :MLA_NOPE_DIM]
    w_uk_p = jnp.concatenate([uk, jnp.zeros_like(uk)], axis=-1).reshape(MLA_KV_LORA, -1)
    w_uv = ukv[..., MLA_NOPE_DIM:].reshape(MLA_KV_LORA, -1).T
    bf = jnp.bfloat16
    return w_in_p.astype(bf), w_uq_p.astype(bf), w_uk_p.astype(bf), w_uv.astype(bf)


def _rope_freq_slab():
    inv_freq = ROPE_THETA ** (-jnp.arange(0, MLA_ROPE_DIM, 2, dtype=jnp.float32) / MLA_ROPE_DIM)
    zeros = jnp.zeros((MLA_NOPE_DIM,), jnp.float32)
    pad = jnp.zeros((HEAD_SLAB - MLA_NOPE_DIM - MLA_ROPE_DIM,), jnp.float32)
    return jnp.concatenate([zeros, inv_freq, inv_freq, pad])[None, :]


def _mla_proj(x, pos, g, w_in_p, q_norm, kv_norm, w_uq_p, w_uk_p, w_uv):
    tm = MLA_PROJ_TOKENS
    chunks = tm // MLA_V_CHUNK
    t = x.shape[0]
    width = MLA_HEADS * HEAD_SLAB
    vwidth = MLA_HEADS * MLA_V_DIM
    row = lambda i: (i, 0)
    return pl.pallas_call(
        _mla_proj_kernel,
        grid=(t // tm,),
        in_specs=[
            pl.BlockSpec((tm, D_MODEL), row),
            pl.BlockSpec((tm, 1), row),
            _const_spec((1, HEAD_SLAB)),
            _const_spec((1, D_MODEL)),
            _const_spec(w_in_p.shape),
            _const_spec((1, MLA_Q_LORA)),
            _const_spec((1, MLA_KV_LORA)),
            _const_spec(w_uq_p.shape),
            _const_spec(w_uk_p.shape),
            _const_spec(w_uv.shape),
        ],
        out_specs=[
            pl.BlockSpec((tm, width), row),
            pl.BlockSpec((tm, width), row),
            pl.BlockSpec((chunks, vwidth, MLA_V_CHUNK), lambda i: (i, 0, 0)),
        ],
        out_shape=[
            jax.ShapeDtypeStruct((t, width), jnp.bfloat16),
            jax.ShapeDtypeStruct((t, width), jnp.bfloat16),
            jax.ShapeDtypeStruct((t // MLA_V_CHUNK, vwidth, MLA_V_CHUNK), jnp.bfloat16),
        ],
        compiler_params=_params(("parallel",)),
        name="mla_proj",
    )(x, pos, _rope_freq_slab(), g, w_in_p, q_norm, kv_norm, w_uq_p, w_uk_p, w_uv)


def _mla_attn_kernel(q_ref, k_ref, v_ref, o_ref,
                     s0_ref, s1_ref, p0_ref, p1_ref, mx_ref, m_ref, alpha_ref, acc_ref):
    qi = pl.program_id(2)
    tq, tk = MLA_Q_BLOCK, MLA_K_BLOCK
    chunks = tk // MLA_V_CHUNK
    heads = range(HEADS_PER_STEP)
    q_heads = [q_ref[0, :, h * HEAD_SLAB:(h + 1) * HEAD_SLAB] for h in heads]
    ones = jnp.ones((MLA_DENOM_ROWS, tk), jnp.bfloat16)
    ratio = tq // tk
    n_full = qi * ratio
    s_bufs = (s0_ref, s1_ref)
    p_bufs = (p0_ref, p1_ref)

    def scores(j, parity, first=0):
        start = pl.multiple_of(j * tk, tk)
        for h in heads:
            k_blk = k_ref[0, pl.ds(start, tk), h * HEAD_SLAB:(h + 1) * HEAD_SLAB]
            s = _dot_nt(k_blk, q_heads[h][first:])
            s_bufs[parity][h, :, first:tq] = s
            mx_ref[parity, h, :, first:] = jnp.max(s, axis=0, keepdims=True)

    def values(j, parity, first=0):
        for h in heads:
            v_t = jnp.concatenate(
                [v_ref[0, j * chunks + ci, h * MLA_V_DIM:(h + 1) * MLA_V_DIM, :]
                 for ci in range(chunks)], axis=1)
            pv = _dot(jnp.concatenate([v_t, ones], axis=0), p_bufs[parity][h, :, first:tq])
            acc_ref[h, :, first:] = alpha_ref[parity, h, :, first:] * acc_ref[h, :, first:] + pv

    def softmax(parity, mask, first=0, last=tq):
        cols = slice(first, last)
        for h in heads:
            s = s_bufs[parity][h, :, cols]
            if mask is None:
                block_max = mx_ref[parity, h, :, cols]
            else:
                s = jnp.where(mask, s, -jnp.inf)
                block_max = jnp.max(s, axis=0, keepdims=True)
            m_old = m_ref[h, :, cols]
            m_new = jnp.maximum(m_old, block_max)
            p_bufs[parity][h, :, cols] = jnp.exp2(s - m_new).astype(jnp.bfloat16)
            alpha_ref[parity, h, :, cols] = jnp.exp2(m_old - m_new)
            m_ref[h, :, cols] = m_new

    def step(i, parity):
        softmax(parity, None)
        scores(i + 1, 1 - parity)
        values(jnp.maximum(i - 1, 0), 1 - parity)

    def diagonal_steps():
        key = lax.broadcasted_iota(jnp.int32, (tk, tk), 0)
        qry = lax.broadcasted_iota(jnp.int32, (tk, tk), 1)
        for off in range(ratio):
            if off + 1 < ratio:
                scores(n_full + off + 1, (off + 1) % 2, (off + 1) * tk)
            softmax(off % 2, key <= qry, off * tk, (off + 1) * tk)
            if off + 1 < ratio:
                softmax(off % 2, None, (off + 1) * tk)
            values(jnp.maximum(n_full + off - 1, 0), (off + 1) % 2, max(off - 1, 0) * tk)
        values(n_full + ratio - 1, (ratio - 1) % 2, (ratio - 1) * tk)
        o_t = jnp.concatenate(
            [acc_ref[h, :MLA_V_DIM] / acc_ref[h, MLA_V_DIM:MLA_V_DIM + 1] for h in heads], axis=0)
        o_ref[0] = o_t.T.astype(o_ref.dtype)

    m_ref[...] = jnp.full(m_ref.shape, -jnp.inf, jnp.float32)
    alpha_ref[...] = jnp.ones(alpha_ref.shape, jnp.float32)
    acc_ref[...] = jnp.zeros(acc_ref.shape, jnp.float32)

    p1_ref[...] = jnp.zeros(p1_ref.shape, jnp.bfloat16)
    scores(0, 0)

    def two_steps(ii, carry):
        step(2 * ii, 0)
        step(2 * ii + 1, 1)
        return carry

    lax.fori_loop(0, n_full // 2, two_steps, 0)
    diagonal_steps()


def _mla_attention(q, k, v, batch, seq):
    assert MLA_Q_BLOCK % (2 * MLA_K_BLOCK) == 0
    assert seq % MLA_Q_BLOCK == 0 and MLA_K_BLOCK % MLA_V_CHUNK == 0
    pairs = MLA_HEADS // HEADS_PER_STEP
    nq = seq // MLA_Q_BLOCK
    qk_w = HEADS_PER_STEP * HEAD_SLAB
    return pl.pallas_call(
        _mla_attn_kernel,
        grid=(batch, pairs, nq),
        in_specs=[
            pl.BlockSpec((1, MLA_Q_BLOCK, qk_w), lambda b, p, i: (b, i, p)),
            pl.BlockSpec((1, seq, qk_w), lambda b, p, i: (b, 0, p), pipeline_mode=pl.Buffered(1)),
            pl.BlockSpec((1, seq // MLA_V_CHUNK, HEADS_PER_STEP * MLA_V_DIM, MLA_V_CHUNK),
                         lambda b, p, i: (b, 0, p, 0), pipeline_mode=pl.Buffered(1)),
        ],
        out_specs=pl.BlockSpec((1, MLA_Q_BLOCK, HEADS_PER_STEP * MLA_V_DIM),
                               lambda b, p, i: (b, i, p)),
        out_shape=jax.ShapeDtypeStruct((batch, seq, MLA_HEADS * MLA_V_DIM), jnp.bfloat16),
        scratch_shapes=[
            pltpu.VMEM((HEADS_PER_STEP, MLA_K_BLOCK, MLA_Q_BLOCK), jnp.float32),
            pltpu.VMEM((HEADS_PER_STEP, MLA_K_BLOCK, MLA_Q_BLOCK), jnp.float32),
            pltpu.VMEM((HEADS_PER_STEP, MLA_K_BLOCK, MLA_Q_BLOCK), jnp.bfloat16),
            pltpu.VMEM((HEADS_PER_STEP, MLA_K_BLOCK, MLA_Q_BLOCK), jnp.bfloat16),
            pltpu.VMEM((2, HEADS_PER_STEP, 1, MLA_Q_BLOCK), jnp.float32),
            pltpu.VMEM((HEADS_PER_STEP, 1, MLA_Q_BLOCK), jnp.float32),
            pltpu.VMEM((2, HEADS_PER_STEP, 1, MLA_Q_BLOCK), jnp.float32),
            pltpu.VMEM((HEADS_PER_STEP, MLA_V_DIM + MLA_DENOM_ROWS, MLA_Q_BLOCK), jnp.float32),
        ],
        compiler_params=_params(("parallel", "parallel", "arbitrary")),
        name="mla_attn",
    )(q, k, v)


def kernel(x, positions, norm_g, ffn_w_in, ffn_w_out, sb_w_in, sb_w_out, mla_w_in,
           mla_q_norm, mla_w_uq, mla_kv_norm, mla_w_ukv, mla_w_out):
    batch, seq, _ = x.shape
    t = batch * seq
    bf = jnp.bfloat16
    depth = norm_g.shape[0]
    xt = x.reshape(t, D_MODEL)
    pos = positions.reshape(t, 1)
    ffn_in = ffn_w_in.astype(bf)
    ffn_out = ffn_w_out.astype(bf)
    for i in range(depth):
        g = norm_g[i][:, None, :]
        xt = _ffn(xt, g[0], g[1], ffn_in, ffn_out, i, 0)
        j = i // 2
        if i % 2 == 0:
            width = SB_HEADS * SB_HEAD_DIM
            w_qk = sb_w_in[j][:, :2 * width].astype(bf)
            w_vt = sb_w_in[j][:, 2 * width:].T.astype(bf)
            qk, v_t = _sb_qkv(xt, g[2], w_qk, w_vt)
            o = _sb_attention(qk.reshape(batch, seq, -1),
                              v_t.reshape(batch, seq // SB_BLOCK, width, SB_BLOCK), batch, seq)
            w_out = sb_w_out[j]
        else:
            w_in_p, w_uq_p, w_uk_p, w_uv = _mla_weights(mla_w_in[j], mla_w_uq[j], mla_w_ukv[j])
            q, k, v = _mla_proj(xt, pos, g[2], w_in_p, mla_q_norm[j][None, :],
                                mla_kv_norm[j][None, :], w_uq_p, w_uk_p, w_uv)
            v_t = v.reshape(batch, seq // MLA_V_CHUNK, MLA_HEADS * MLA_V_DIM, MLA_V_CHUNK)
            o = _mla_attention(q.reshape(batch, seq, -1), k.reshape(batch, seq, -1),
                               v_t, batch, seq)
            w_out = mla_w_out[j]
        xt = _mixer_out_ffn(xt, o.reshape(t, -1), w_out.astype(bf), g[3], g[4], g[5],
                            ffn_in, ffn_out, i, 1)
    return xt.reshape(batch, seq, D_MODEL)
```

```python
import functools
import math

import jax
import jax.numpy as jnp
from jax import lax
from jax.experimental import pallas as pl
from jax.experimental.pallas import tpu as pltpu

D_MODEL = 1024
EPS = 1e-6
D_FF = 2816
FFN_RESIDUAL_WEIGHT = 0.5

SB_HEADS = 16
SB_HEAD_DIM = 64

MLA_HEADS = 16
MLA_Q_LORA = 256
MLA_KV_LORA = 128
MLA_NOPE_DIM = 64
MLA_ROPE_DIM = 32
MLA_V_DIM = 64
ROPE_THETA = 10000.0

LANES = 128
HEAD_SLAB = 128
HEADS_PER_STEP = 4
VMEM_LIMIT_BYTES = 56 * 1024 * 1024

SB_TAIL_CUTOFF = 104.0
SB_BLOCK = 128
SB_HEADS_PER_STEP = 16

MLA_Q_BLOCK = 1024
MLA_K_BLOCK = 512
MLA_PROJ_TOKENS = 512
MLA_V_CHUNK = 256
MLA_SCORE_SCALE = math.log2(math.e) / math.sqrt(MLA_NOPE_DIM + MLA_ROPE_DIM)
MLA_DENOM_ROWS = 16


def _rms(x, g):
    return x * lax.rsqrt(jnp.mean(x * x, axis=-1, keepdims=True) + EPS) * g


def _dot(a, b):
    return jnp.dot(a, b, preferred_element_type=jnp.float32)


def _dot_nt(a, b):
    return lax.dot_general(a, b, (((1,), (1,)), ((), ())),
                           preferred_element_type=jnp.float32)


def _params(semantics):
    return pltpu.CompilerParams(dimension_semantics=semantics,
                                vmem_limit_bytes=VMEM_LIMIT_BYTES)


def _const_spec(shape):
    return pl.BlockSpec(shape, lambda *_: (0,) * len(shape),
                        pipeline_mode=pl.Buffered(1))


FFN_TOKENS = 512


FFN_SPLIT = 2


def _ffn_block(xs, gpre_ref, gpost_ref, win_ref, wout_ref):
    xn = [_rms(x, gpre_ref[...]).astype(jnp.bfloat16) for x in xs]
    h = [_dot(v, win_ref[...]) for v in xn]
    act = [(v[:, :D_FF] * jax.nn.sigmoid(v[:, :D_FF]) * v[:, D_FF:]).astype(jnp.bfloat16)
           for v in h]
    f = [_dot(v, wout_ref[...]) for v in act]
    return jnp.concatenate(
        [x + FFN_RESIDUAL_WEIGHT * _rms(fx, gpost_ref[...]) for x, fx in zip(xs, f)], axis=0)


def _sub_tiles(ref):
    rows = ref.shape[0] // FFN_SPLIT
    return [ref[k * rows:(k + 1) * rows, :] for k in range(FFN_SPLIT)]


def _ffn_kernel(x_ref, gpre_ref, gpost_ref, win_ref, wout_ref, y_ref):
    y_ref[...] = _ffn_block(_sub_tiles(x_ref), gpre_ref, gpost_ref, win_ref, wout_ref)


def _mixer_out_ffn_kernel(x_ref, o_ref, wo_ref, gmix_ref, gpre_ref, gpost_ref, win_ref, wout_ref,
                          y_ref):
    mixed = [_dot(o, wo_ref[...]) for o in _sub_tiles(o_ref)]
    xs = [x + _rms(m, gmix_ref[...]) for x, m in zip(_sub_tiles(x_ref), mixed)]
    y_ref[...] = _ffn_block(xs, gpre_ref, gpost_ref, win_ref, wout_ref)


def _ffn_specs(layer, half):
    def weight(rows, cols):
        return pl.BlockSpec((None, None, rows, cols), lambda *_: (layer, half, 0, 0),
                            pipeline_mode=pl.Buffered(1))
    return [
        _const_spec((1, D_MODEL)),
        _const_spec((1, D_MODEL)),
        weight(D_MODEL, 2 * D_FF),
        weight(D_FF, D_MODEL),
    ]


def _ffn(x, g_pre, g_post, w_in, w_out, layer, half):
    t = x.shape[0]
    rows = pl.BlockSpec((FFN_TOKENS, D_MODEL), lambda i: (i, 0))
    return pl.pallas_call(
        _ffn_kernel,
        grid=(t // FFN_TOKENS,),
        in_specs=[rows] + _ffn_specs(layer, half),
        out_specs=rows,
        out_shape=jax.ShapeDtypeStruct((t, D_MODEL), jnp.float32),
        compiler_params=_params(("parallel",)),
        name="ffn",
    )(x, g_pre, g_post, w_in, w_out)


def _mixer_out_ffn(x, o, w_o, g_mix, g_pre, g_post, w_in, w_out, layer, half):
    t = x.shape[0]
    rows = pl.BlockSpec((FFN_TOKENS, D_MODEL), lambda i: (i, 0))
    return pl.pallas_call(
        _mixer_out_ffn_kernel,
        grid=(t // FFN_TOKENS,),
        in_specs=[rows, rows, _const_spec((D_MODEL, D_MODEL)), _const_spec((1, D_MODEL))]
        + _ffn_specs(layer, half),
        out_specs=rows,
        out_shape=jax.ShapeDtypeStruct((t, D_MODEL), jnp.float32),
        compiler_params=_params(("parallel",)),
        name="mixer_out_ffn",
    )(x, o, w_o, g_mix, g_pre, g_post, w_in, w_out)


def _sb_qkv_kernel(x_ref, g_ref, wqk_ref, wvt_ref, qk_ref, vt_ref):
    h = _rms(x_ref[...], g_ref[...]).astype(jnp.bfloat16)
    width = SB_HEADS * SB_HEAD_DIM
    y = _dot(h, wqk_ref[...])
    qk_ref[:, :width] = (y[:, :width] * (1.0 / math.sqrt(SB_HEAD_DIM))).astype(qk_ref.dtype)
    qk_ref[:, width:] = y[:, width:].astype(qk_ref.dtype)
    v_t = _dot_nt(wvt_ref[...], h).astype(vt_ref.dtype)
    for ci in range(vt_ref.shape[0]):
        vt_ref[ci] = v_t[:, ci * SB_BLOCK:(ci + 1) * SB_BLOCK]


def _sb_qkv(x, g, w_qk, w_vt, tm=512):
    t = x.shape[0]
    width = SB_HEADS * SB_HEAD_DIM
    chunks = tm // SB_BLOCK
    return pl.pallas_call(
        _sb_qkv_kernel,
        grid=(t // tm,),
        in_specs=[
            pl.BlockSpec((tm, D_MODEL), lambda i: (i, 0)),
            _const_spec((1, D_MODEL)),
            _const_spec((D_MODEL, 2 * width)),
            _const_spec((width, D_MODEL)),
        ],
        out_specs=[
            pl.BlockSpec((tm, 2 * width), lambda i: (i, 0)),
            pl.BlockSpec((chunks, width, SB_BLOCK), lambda i: (i, 0, 0)),
        ],
        out_shape=[
            jax.ShapeDtypeStruct((t, 2 * width), jnp.bfloat16),
            jax.ShapeDtypeStruct((t // SB_BLOCK, width, SB_BLOCK), jnp.bfloat16),
        ],
        compiler_params=_params(("parallel",)),
        name="sb_qkv",
    )(x, g, w_qk, w_vt)


def _sb_attn_kernel(q_ref, k_ref, v_ref, o_ref):
    qi = pl.program_id(2)
    blk = SB_BLOCK
    heads = range(SB_HEADS_PER_STEP)
    lane = lax.broadcasted_iota(jnp.int32, (blk, LANES), 1)
    key = lax.broadcasted_iota(jnp.int32, (blk, blk), 0)
    qry = lax.broadcasted_iota(jnp.int32, (blk, blk), 1)
    causal = key < qry
    half = blk // 2

    def cum_matrix(tk):
        ur = lax.broadcasted_iota(jnp.int32, (tk, 2 * tk), 0)
        uc = lax.broadcasted_iota(jnp.int32, (tk, 2 * tk), 1)
        return jnp.where((uc % tk) > ur, 1.0, 0.0).astype(jnp.bfloat16)

    cum = {blk: cum_matrix(blk), half: cum_matrix(half)}
    slabs = [slice((h // 2) * LANES, (h // 2 + 1) * LANES) for h in heads]
    q_heads = []
    for h in heads:
        q = q_ref[0, :, slabs[h]]
        q_heads.append(jnp.where((lane // SB_HEAD_DIM) == h % 2, q, jnp.zeros_like(q)))

    def logits(start, tk, h):
        start = pl.multiple_of(start, tk)
        return _dot_nt(k_ref[0, pl.ds(start, tk), slabs[h]], q_heads[h])

    def softplus_parts(z, mask):
        sp = jnp.maximum(z, 0.0) + jnp.log(1.0 + jnp.exp2(jnp.abs(z) * (-math.log2(math.e))))
        sp_m = sp if mask is None else jnp.where(mask, sp, 0.0)
        hi = sp_m.astype(jnp.bfloat16)
        lo = (sp_m - hi.astype(jnp.float32)).astype(jnp.bfloat16)
        return sp, jnp.concatenate([hi, lo], axis=0), jnp.sum(sp_m, axis=0, keepdims=True)

    def weights(z, sp, inner, tail, mask):
        w = jnp.exp(z - sp - inner - tail)
        return (w if mask is None else jnp.where(mask, w, 0.0)).astype(jnp.bfloat16)

    def values(v_t, a, valid=None):
        if valid is not None:
            v_t = jnp.where(valid, v_t, jnp.zeros_like(v_t))
        return _dot(v_t, a)

    def head_rows(h):
        return slice(h * SB_HEAD_DIM, (h + 1) * SB_HEAD_DIM)

    b1 = jnp.maximum(qi - 1, 0)
    b2 = jnp.maximum(qi - 2, 0)
    spans = [(qi * blk, blk, qi, slice(0, blk), causal, None),
             (b1 * blk, blk, b1, slice(0, blk), None, qi >= 1),
             (b2 * blk + half, half, b2, slice(half, blk), None, qi >= 2)]
    tiles = [(d, h) for d in range(len(spans)) for h in heads]
    z = {(d, h): logits(spans[d][0], spans[d][1], h) for d, h in tiles}
    parts = {t: softplus_parts(z[t], spans[t[0]][4]) for t in tiles}
    inner = {t: _dot(cum[spans[t[0]][1]], parts[t][1]) for t in tiles}
    tails = [jnp.zeros((1, blk), jnp.float32) for _ in heads]
    a = {}
    for d, h in tiles:
        valid = spans[d][5]
        a[d, h] = weights(z[d, h], parts[d, h][0], inner[d, h], tails[h], spans[d][4])
        total = parts[d, h][2]
        tails[h] = tails[h] + (total if valid is None else jnp.where(valid, total, 0.0))
    accs = [sum(values(v_ref[0, chunk, head_rows(h), lanes], a[d, h], valid)
                for d, (_, _, chunk, lanes, _, valid) in enumerate(spans)) for h in heads]

    def live(state):
        smallest = functools.reduce(jnp.minimum, [state[1 + 2 * h] for h in heads])
        return jnp.logical_and(state[0] >= 0, jnp.min(smallest) < SB_TAIL_CUTOFF)

    def older(state):
        hb = state[0]
        newer_half = hb % 2 == 1
        out = [hb - 1]
        for h in heads:
            tail, acc = state[1 + 2 * h], state[2 + 2 * h]
            zz = logits(hb * half, half, h)
            sp, hilo, total = softplus_parts(zz, None)
            v_blk = v_ref[0, hb // 2, head_rows(h), :]
            v_t = jnp.where(newer_half, v_blk[:, half:], v_blk[:, :half])
            acc = acc + values(v_t, weights(zz, sp, _dot(cum[half], hilo), tail, None))
            out += [tail + total, acc]
        return tuple(out)

    init = [2 * qi - 4]
    for h in heads:
        init += [tails[h], accs[h]]
    state = lax.while_loop(live, older, tuple(init))
    o_t = jnp.concatenate([state[2 + 2 * h] for h in heads], axis=0)
    o_ref[0] = o_t.T.astype(o_ref.dtype)


def _sb_attention(qk, v_t, batch, seq):
    groups = SB_HEADS // SB_HEADS_PER_STEP
    width = SB_HEADS_PER_STEP * SB_HEAD_DIM
    nq = seq // SB_BLOCK
    return pl.pallas_call(
        _sb_attn_kernel,
        grid=(batch, groups, nq),
        in_specs=[
            pl.BlockSpec((1, SB_BLOCK, width), lambda b, p, i: (b, i, p)),
            pl.BlockSpec((1, seq, width), lambda b, p, i: (b, 0, groups + p),
                         pipeline_mode=pl.Buffered(1)),
            pl.BlockSpec((1, nq, width, SB_BLOCK), lambda b, p, i: (b, 0, p, 0),
                         pipeline_mode=pl.Buffered(1)),
        ],
        out_specs=pl.BlockSpec((1, SB_BLOCK, width), lambda b, p, i: (b, i, p)),
        out_shape=jax.ShapeDtypeStruct((batch, seq, SB_HEADS * SB_HEAD_DIM), jnp.bfloat16),
        compiler_params=_params(("parallel", "parallel", "arbitrary")),
        name="sb_attn",
    )(qk, qk, v_t)


def _mla_proj_kernel(x_ref, pos_ref, freq_ref, g_ref, win_ref, qn_ref, kvn_ref,
                     wuq_ref, wuk_ref, wuv_ref, q_ref, k_ref, v_ref):
    h = _rms(x_ref[...], g_ref[...]).astype(jnp.bfloat16)
    proj = _dot(h, win_ref[...])
    c_q = proj[:, :MLA_Q_LORA]
    c_kv = proj[:, MLA_Q_LORA:MLA_Q_LORA + MLA_KV_LORA]
    kr = proj[:, MLA_Q_LORA + MLA_KV_LORA:MLA_Q_LORA + MLA_KV_LORA + HEAD_SLAB]
    kr_rot = proj[:, MLA_Q_LORA + MLA_KV_LORA + HEAD_SLAB:]

    tm = x_ref.shape[0]
    lane = lax.broadcasted_iota(jnp.int32, (tm, HEAD_SLAB), 1)
    ang = pos_ref[...].astype(jnp.float32) * freq_ref[...]
    is_rope = jnp.logical_and(lane >= MLA_NOPE_DIM, lane < MLA_NOPE_DIM + MLA_ROPE_DIM)
    cos = jnp.where(lane < MLA_NOPE_DIM, 1.0, jnp.where(is_rope, jnp.cos(ang), 0.0))
    sin = jnp.where(is_rope, jnp.sin(ang), 0.0)

    qn = _rms(c_q, qn_ref[...]).astype(jnp.bfloat16)
    q_all = _dot(qn, wuq_ref[...])
    kvn = _rms(c_kv, kvn_ref[...]).astype(jnp.bfloat16)
    k_nope = _dot(kvn, wuk_ref[...])
    v_t = _dot_nt(wuv_ref[...], kvn).astype(v_ref.dtype)
    for ci in range(v_ref.shape[0]):
        v_ref[ci] = v_t[:, ci * MLA_V_CHUNK:(ci + 1) * MLA_V_CHUNK]
    k_rope = kr * cos + kr_rot * sin
    cos_q = cos * MLA_SCORE_SCALE
    sin_q = sin * MLA_SCORE_SCALE
    width = MLA_HEADS * HEAD_SLAB
    for hd in range(MLA_HEADS):
        sl = slice(hd * HEAD_SLAB, (hd + 1) * HEAD_SLAB)
        rot = slice(width + hd * HEAD_SLAB, width + (hd + 1) * HEAD_SLAB)
        q_ref[:, sl] = (q_all[:, sl] * cos_q + q_all[:, rot] * sin_q).astype(q_ref.dtype)
        k_ref[:, sl] = (k_nope[:, sl] + k_rope).astype(k_ref.dtype)


def _mla_weights(w_in, w_uq, w_ukv):
    half = MLA_ROPE_DIM // 2
    pad = HEAD_SLAB - MLA_NOPE_DIM - MLA_ROPE_DIM

    def slab(nope, rope):
        return jnp.concatenate(
            [nope, rope, jnp.zeros(rope.shape[:-1] + (pad,), rope.dtype)], axis=-1)

    def rotate(rope):
        return jnp.concatenate([-rope[..., half:], rope[..., :half]], axis=-1)

    lat = MLA_Q_LORA + MLA_KV_LORA
    kr_w = w_in[:, lat:]
    zeros_in = jnp.zeros((D_MODEL, MLA_NOPE_DIM), w_in.dtype)
    w_in_p = jnp.concatenate(
        [w_in[:, :lat], slab(zeros_in, kr_w), slab(zeros_in, rotate(kr_w))], axis=1)

    uq = w_uq.reshape(MLA_Q_LORA, MLA_HEADS, MLA_NOPE_DIM + MLA_ROPE_DIM)
    uq_nope, uq_rope = uq[..., :MLA_NOPE_DIM], uq[..., MLA_NOPE_DIM:]
    w_uq_p = jnp.concatenate(
        [slab(uq_nope, uq_rope).reshape(MLA_Q_LORA, -1),
         slab(jnp.zeros_like(uq_nope), rotate(uq_rope)).reshape(MLA_Q_LORA, -1)], axis=1)

    ukv = w_ukv.reshape(MLA_KV_LORA, MLA_HEADS, MLA_NOPE_DIM + MLA_V_DIM)
    uk = ukv[..., :MLA_NOPE_DIM]
    w_uk_p = jnp.concatenate([uk, jnp.zeros_like(uk)], axis=-1).reshape(MLA_KV_LORA, -1)
    w_uv = ukv[..., MLA_NOPE_DIM:].reshape(MLA_KV_LORA, -1).T
    bf = jnp.bfloat16
    return w_in_p.astype(bf), w_uq_p.astype(bf), w_uk_p.astype(bf), w_uv.astype(bf)


def _rope_freq_slab():
    inv_freq = ROPE_THETA ** (-jnp.arange(0, MLA_ROPE_DIM, 2, dtype=jnp.float32) / MLA_ROPE_DIM)
    zeros = jnp.zeros((MLA_NOPE_DIM,), jnp.float32)
    pad = jnp.zeros((HEAD_SLAB - MLA_NOPE_DIM - MLA_ROPE_DIM,), jnp.float32)
    return jnp.concatenate([zeros, inv_freq, inv_freq, pad])[None, :]


def _mla_proj(x, pos, g, w_in_p, q_norm, kv_norm, w_uq_p, w_uk_p, w_uv):
    tm = MLA_PROJ_TOKENS
    chunks = tm // MLA_V_CHUNK
    t = x.shape[0]
    width = MLA_HEADS * HEAD_SLAB
    vwidth = MLA_HEADS * MLA_V_DIM
    row = lambda i: (i, 0)
    return pl.pallas_call(
        _mla_proj_kernel,
        grid=(t // tm,),
        in_specs=[
            pl.BlockSpec((tm, D_MODEL), row),
            pl.BlockSpec((tm, 1), row),
            _const_spec((1, HEAD_SLAB)),
            _const_spec((1, D_MODEL)),
            _const_spec(w_in_p.shape),
            _const_spec((1, MLA_Q_LORA)),
            _const_spec((1, MLA_KV_LORA)),
            _const_spec(w_uq_p.shape),
            _const_spec(w_uk_p.shape),
            _const_spec(w_uv.shape),
        ],
        out_specs=[
            pl.BlockSpec((tm, width), row),
            pl.BlockSpec((tm, width), row),
            pl.BlockSpec((chunks, vwidth, MLA_V_CHUNK), lambda i: (i, 0, 0)),
        ],
        out_shape=[
            jax.ShapeDtypeStruct((t, width), jnp.bfloat16),
            jax.ShapeDtypeStruct((t, width), jnp.bfloat16),
            jax.ShapeDtypeStruct((t // MLA_V_CHUNK, vwidth, MLA_V_CHUNK), jnp.bfloat16),
        ],
        compiler_params=_params(("parallel",)),
        name="mla_proj",
    )(x, pos, _rope_freq_slab(), g, w_in_p, q_norm, kv_norm, w_uq_p, w_uk_p, w_uv)


def _mla_attn_kernel(q_ref, k_ref, v_ref, o_ref,
                     s0_ref, s1_ref, p0_ref, p1_ref, mx_ref, m_ref, alpha_ref, acc_ref):
    qi = pl.program_id(2)
    tq, tk = MLA_Q_BLOCK, MLA_K_BLOCK
    chunks = tk // MLA_V_CHUNK
    heads = range(HEADS_PER_STEP)
    q_heads = [q_ref[0, :, h * HEAD_SLAB:(h + 1) * HEAD_SLAB] for h in heads]
    ones = jnp.ones((MLA_DENOM_ROWS, tk), jnp.bfloat16)
    ratio = tq // tk
    n_full = qi * ratio
    s_bufs = (s0_ref, s1_ref)
    p_bufs = (p0_ref, p1_ref)

    def scores(j, parity, first=0):
        start = pl.multiple_of(j * tk, tk)
        for h in heads:
            k_blk = k_ref[0, pl.ds(start, tk), h * HEAD_SLAB:(h + 1) * HEAD_SLAB]
            s = _dot_nt(k_blk, q_heads[h][first:])
            s_bufs[parity][h, :, first:tq] = s
            mx_ref[parity, h, :, first:] = jnp.max(s, axis=0, keepdims=True)

    def values(j, parity, first=0):
        for h in heads:
            v_t = jnp.concatenate(
                [v_ref[0, j * chunks + ci, h * MLA_V_DIM:(h + 1) * MLA_V_DIM, :]
                 for ci in range(chunks)], axis=1)
            pv = _dot(jnp.concatenate([v_t, ones], axis=0), p_bufs[parity][h, :, first:tq])
            acc_ref[h, :, first:] = alpha_ref[parity, h, :, first:] * acc_ref[h, :, first:] + pv

    def softmax(parity, mask, first=0, last=tq):
        cols = slice(first, last)
        for h in heads:
            s = s_bufs[parity][h, :, cols]
            if mask is None:
                block_max = mx_ref[parity, h, :, cols]
            else:
                s = jnp.where(mask, s, -jnp.inf)
                block_max = jnp.max(s, axis=0, keepdims=True)
            m_old = m_ref[h, :, cols]
            m_new = jnp.maximum(m_old, block_max)
            p_bufs[parity][h, :, cols] = jnp.exp2(s - m_new).astype(jnp.bfloat16)
            alpha_ref[parity, h, :, cols] = jnp.exp2(m_old - m_new)
            m_ref[h, :, cols] = m_new

    def step(i, parity):
        softmax(parity, None)
        scores(i + 1, 1 - parity)
        values(jnp.maximum(i - 1, 0), 1 - parity)

    def diagonal_steps():
        key = lax.broadcasted_iota(jnp.int32, (tk, tk), 0)
        qry = lax.broadcasted_iota(jnp.int32, (tk, tk), 1)
        for off in range(ratio):
            if off + 1 < ratio:
                scores(n_full + off + 1, (off + 1) % 2, (off + 1) * tk)
            softmax(off % 2, key <= qry, off * tk, (off + 1) * tk)
            if off + 1 < ratio:
                softmax(off % 2, None, (off + 1) * tk)
            values(jnp.maximum(n_full + off - 1, 0), (off + 1) % 2, max(off - 1, 0) * tk)
        values(n_full + ratio - 1, (ratio - 1) % 2, (ratio - 1) * tk)
        o_t = jnp.concatenate(
            [acc_ref[h, :MLA_V_DIM] / acc_ref[h, MLA_V_DIM:MLA_V_DIM + 1] for h in heads], axis=0)
        o_ref[0] = o_t.T.astype(o_ref.dtype)

    m_ref[...] = jnp.full(m_ref.shape, -jnp.inf, jnp.float32)
    alpha_ref[...] = jnp.ones(alpha_ref.shape, jnp.float32)
    acc_ref[...] = jnp.zeros(acc_ref.shape, jnp.float32)

    p1_ref[...] = jnp.zeros(p1_ref.shape, jnp.bfloat16)
    scores(0, 0)

    def two_steps(ii, carry):
        step(2 * ii, 0)
        step(2 * ii + 1, 1)
        return carry

    lax.fori_loop(0, n_full // 2, two_steps, 0)
    diagonal_steps()


def _mla_attention(q, k, v, batch, seq):
    assert MLA_Q_BLOCK % (2 * MLA_K_BLOCK) == 0
    assert seq % MLA_Q_BLOCK == 0 and MLA_K_BLOCK % MLA_V_CHUNK == 0
    pairs = MLA_HEADS // HEADS_PER_STEP
    nq = seq // MLA_Q_BLOCK
    qk_w = HEADS_PER_STEP * HEAD_SLAB
    return pl.pallas_call(
        _mla_attn_kernel,
        grid=(batch, pairs, nq),
        in_specs=[
            pl.BlockSpec((1, MLA_Q_BLOCK, qk_w), lambda b, p, i: (b, i, p)),
            pl.BlockSpec((1, seq, qk_w), lambda b, p, i: (b, 0, p), pipeline_mode=pl.Buffered(1)),
            pl.BlockSpec((1, seq // MLA_V_CHUNK, HEADS_PER_STEP * MLA_V_DIM, MLA_V_CHUNK),
                         lambda b, p, i: (b, 0, p, 0), pipeline_mode=pl.Buffered(1)),
        ],
        out_specs=pl.BlockSpec((1, MLA_Q_BLOCK, HEADS_PER_STEP * MLA_V_DIM),
                               lambda b, p, i: (b, i, p)),
        out_shape=jax.ShapeDtypeStruct((batch, seq, MLA_HEADS * MLA_V_DIM), jnp.bfloat16),
        scratch_shapes=[
            pltpu.VMEM((HEADS_PER_STEP, MLA_K_BLOCK, MLA_Q_BLOCK), jnp.float32),
            pltpu.VMEM((HEADS_PER_STEP, MLA_K_BLOCK, MLA_Q_BLOCK), jnp.float32),
            pltpu.VMEM((HEADS_PER_STEP, MLA_K_BLOCK, MLA_Q_BLOCK), jnp.bfloat16),
            pltpu.VMEM((HEADS_PER_STEP, MLA_K_BLOCK, MLA_Q_BLOCK), jnp.bfloat16),
            pltpu.VMEM((2, HEADS_PER_STEP, 1, MLA_Q_BLOCK), jnp.float32),
            pltpu.VMEM((HEADS_PER_STEP, 1, MLA_Q_BLOCK), jnp.float32),
            pltpu.VMEM((2, HEADS_PER_STEP, 1, MLA_Q_BLOCK), jnp.float32),
            pltpu.VMEM((HEADS_PER_STEP, MLA_V_DIM + MLA_DENOM_ROWS, MLA_Q_BLOCK), jnp.float32),
        ],
        compiler_params=_params(("parallel", "parallel", "arbitrary")),
        name="mla_attn",
    )(q, k, v)


def kernel(x, positions, norm_g, ffn_w_in, ffn_w_out, sb_w_in, sb_w_out, mla_w_in,
           mla_q_norm, mla_w_uq, mla_kv_norm, mla_w_ukv, mla_w_out):
    batch, seq, _ = x.shape
    t = batch * seq
    bf = jnp.bfloat16
    depth = norm_g.shape[0]
    xt = x.reshape(t, D_MODEL)
    pos = positions.reshape(t, 1)
    ffn_in = ffn_w_in.astype(bf)
    ffn_out = ffn_w_out.astype(bf)
    for i in range(depth):
        g = norm_g[i][:, None, :]
        xt = _ffn(xt, g[0], g[1], ffn_in, ffn_out, i, 0)
        j = i // 2
        if i % 2 == 0:
            width = SB_HEADS * SB_HEAD_DIM
            w_qk = sb_w_in[j][:, :2 * width].astype(bf)
            w_vt = sb_w_in[j][:, 2 * width:].T.astype(bf)
            qk, v_t = _sb_qkv(xt, g[2], w_qk, w_vt)
            o = _sb_attention(qk.reshape(batch, seq, -1),
                              v_t.reshape(batch, seq // SB_BLOCK, width, SB_BLOCK), batch, seq)
            w_out = sb_w_out[j]
        else:
            w_in_p, w_uq_p, w_uk_p, w_uv = _mla_weights(mla_w_in[j], mla_w_uq[j], mla_w_ukv[j])
            q, k, v = _mla_proj(xt, pos, g[2], w_in_p, mla_q_norm[j][None, :],
                                mla_kv_norm[j][None, :], w_uq_p, w_uk_p, w_uv)
            v_t = v.reshape(batch, seq // MLA_V_CHUNK, MLA_HEADS * MLA_V_DIM, MLA_V_CHUNK)
            o = _mla_attention(q.reshape(batch, seq, -1), k.reshape(batch, seq, -1),
                               v_t, batch, seq)
            w_out = mla_w_out[j]
        xt = _mixer_out_ffn(xt, o.reshape(t, -1), w_out.astype(bf), g[3], g[4], g[5],
                            ffn_in, ffn_out, i, 1)
    return xt.reshape(batch, seq, D_MODEL)
```

```python
import functools
import math

import jax
import jax.numpy as jnp
from jax import lax
from jax.experimental import pallas as pl
from jax.experimental.pallas import tpu as pltpu

D_MODEL = 1024
EPS = 1e-6
D_FF = 2816
FFN_RESIDUAL_WEIGHT = 0.5

SB_HEADS = 16
SB_HEAD_DIM = 64

MLA_HEADS = 16
MLA_Q_LORA = 256
MLA_KV_LORA = 128
MLA_NOPE_DIM = 64
MLA_ROPE_DIM = 32
MLA_V_DIM = 64
ROPE_THETA = 10000.0

LANES = 128
HEAD_SLAB = 128
HEADS_PER_STEP = 4
VMEM_LIMIT_BYTES = 56 * 1024 * 1024

SB_TAIL_CUTOFF = 104.0
SB_BLOCK = 128
SB_HEADS_PER_STEP = 16

MLA_Q_BLOCK = 1024
MLA_K_BLOCK = 512
MLA_PROJ_TOKENS = 512
MLA_V_CHUNK = 256
MLA_SCORE_SCALE = math.log2(math.e) / math.sqrt(MLA_NOPE_DIM + MLA_ROPE_DIM)
MLA_DENOM_ROWS = 16


def _rms(x, g):
    return x * lax.rsqrt(jnp.mean(x * x, axis=-1, keepdims=True) + EPS) * g


def _dot(a, b):
    return jnp.dot(a, b, preferred_element_type=jnp.float32)


def _dot_nt(a, b):
    return lax.dot_general(a, b, (((1,), (1,)), ((), ())),
                           preferred_element_type=jnp.float32)


def _params(semantics):
    return pltpu.CompilerParams(dimension_semantics=semantics,
                                vmem_limit_bytes=VMEM_LIMIT_BYTES)


def _const_spec(shape):
    return pl.BlockSpec(shape, lambda *_: (0,) * len(shape),
                        pipeline_mode=pl.Buffered(1))


FFN_TOKENS = 1024


FFN_SPLIT = 4


def _ffn_block(xs, gpre_ref, gpost_ref, win_ref, wout_ref):
    xn = [_rms(x, gpre_ref[...]).astype(jnp.bfloat16) for x in xs]
    h = [_dot(v, win_ref[...]) for v in xn]
    act = [(v[:, :D_FF] * jax.nn.sigmoid(v[:, :D_FF]) * v[:, D_FF:]).astype(jnp.bfloat16)
           for v in h]
    f = [_dot(v, wout_ref[...]) for v in act]
    return jnp.concatenate(
        [x + FFN_RESIDUAL_WEIGHT * _rms(fx, gpost_ref[...]) for x, fx in zip(xs, f)], axis=0)


def _sub_tiles(ref):
    rows = ref.shape[0] // FFN_SPLIT
    return [ref[k * rows:(k + 1) * rows, :] for k in range(FFN_SPLIT)]


def _ffn_kernel(x_ref, gpre_ref, gpost_ref, win_ref, wout_ref, y_ref):
    y_ref[...] = _ffn_block(_sub_tiles(x_ref), gpre_ref, gpost_ref, win_ref, wout_ref)


def _mixer_out_ffn_kernel(x_ref, o_ref, wo_ref, gmix_ref, gpre_ref, gpost_ref, win_ref, wout_ref,
                          y_ref):
    mixed = [_dot(o, wo_ref[...]) for o in _sub_tiles(o_ref)]
    xs = [x + _rms(m, gmix_ref[...]) for x, m in zip(_sub_tiles(x_ref), mixed)]
    y_ref[...] = _ffn_block(xs, gpre_ref, gpost_ref, win_ref, wout_ref)


def _ffn_specs(layer, half):
    def weight(rows, cols):
        return pl.BlockSpec((None, None, rows, cols), lambda *_: (layer, half, 0, 0),
                            pipeline_mode=pl.Buffered(1))
    return [
        _const_spec((1, D_MODEL)),
        _const_spec((1, D_MODEL)),
        weight(D_MODEL, 2 * D_FF),
        weight(D_FF, D_MODEL),
    ]


def _ffn(x, g_pre, g_post, w_in, w_out, layer, half):
    t = x.shape[0]
    rows = pl.BlockSpec((FFN_TOKENS, D_MODEL), lambda i: (i, 0))
    return pl.pallas_call(
        _ffn_kernel,
        grid=(t // FFN_TOKENS,),
        in_specs=[rows] + _ffn_specs(layer, half),
        out_specs=rows,
        out_shape=jax.ShapeDtypeStruct((t, D_MODEL), jnp.float32),
        compiler_params=_params(("parallel",)),
        name="ffn",
    )(x, g_pre, g_post, w_in, w_out)


def _mixer_out_ffn(x, o, w_o, g_mix, g_pre, g_post, w_in, w_out, layer, half):
    t = x.shape[0]
    rows = pl.BlockSpec((FFN_TOKENS, D_MODEL), lambda i: (i, 0))
    return pl.pallas_call(
        _mixer_out_ffn_kernel,
        grid=(t // FFN_TOKENS,),
        in_specs=[rows, rows, _const_spec((D_MODEL, D_MODEL)), _const_spec((1, D_MODEL))]
        + _ffn_specs(layer, half),
        out_specs=rows,
        out_shape=jax.ShapeDtypeStruct((t, D_MODEL), jnp.float32),
        compiler_params=_params(("parallel",)),
        name="mixer_out_ffn",
    )(x, o, w_o, g_mix, g_pre, g_post, w_in, w_out)


def _sb_qkv_kernel(x_ref, g_ref, wqk_ref, wvt_ref, qk_ref, vt_ref):
    h = _rms(x_ref[...], g_ref[...]).astype(jnp.bfloat16)
    width = SB_HEADS * SB_HEAD_DIM
    y = _dot(h, wqk_ref[...])
    qk_ref[:, :width] = (y[:, :width] * (1.0 / math.sqrt(SB_HEAD_DIM))).astype(qk_ref.dtype)
    qk_ref[:, width:] = y[:, width:].astype(qk_ref.dtype)
    v_t = _dot_nt(wvt_ref[...], h).astype(vt_ref.dtype)
    for ci in range(vt_ref.shape[0]):
        vt_ref[ci] = v_t[:, ci * SB_BLOCK:(ci + 1) * SB_BLOCK]


def _sb_qkv(x, g, w_qk, w_vt, tm=512):
    t = x.shape[0]
    width = SB_HEADS * SB_HEAD_DIM
    chunks = tm // SB_BLOCK
    return pl.pallas_call(
        _sb_qkv_kernel,
        grid=(t // tm,),
        in_specs=[
            pl.BlockSpec((tm, D_MODEL), lambda i: (i, 0)),
            _const_spec((1, D_MODEL)),
            _const_spec((D_MODEL, 2 * width)),
            _const_spec((width, D_MODEL)),
        ],
        out_specs=[
            pl.BlockSpec((tm, 2 * width), lambda i: (i, 0)),
            pl.BlockSpec((chunks, width, SB_BLOCK), lambda i: (i, 0, 0)),
        ],
        out_shape=[
            jax.ShapeDtypeStruct((t, 2 * width), jnp.bfloat16),
            jax.ShapeDtypeStruct((t // SB_BLOCK, width, SB_BLOCK), jnp.bfloat16),
        ],
        compiler_params=_params(("parallel",)),
        name="sb_qkv",
    )(x, g, w_qk, w_vt)


def _sb_attn_kernel(q_ref, k_ref, v_ref, o_ref):
    qi = pl.program_id(2)
    blk = SB_BLOCK
    heads = range(SB_HEADS_PER_STEP)
    lane = lax.broadcasted_iota(jnp.int32, (blk, LANES), 1)
    key = lax.broadcasted_iota(jnp.int32, (blk, blk), 0)
    qry = lax.broadcasted_iota(jnp.int32, (blk, blk), 1)
    causal = key < qry
    half = blk // 2

    def cum_matrix(tk):
        ur = lax.broadcasted_iota(jnp.int32, (tk, 2 * tk), 0)
        uc = lax.broadcasted_iota(jnp.int32, (tk, 2 * tk), 1)
        return jnp.where((uc % tk) > ur, 1.0, 0.0).astype(jnp.bfloat16)

    cum = {blk: cum_matrix(blk), half: cum_matrix(half)}
    slabs = [slice((h // 2) * LANES, (h // 2 + 1) * LANES) for h in heads]
    q_heads = []
    for h in heads:
        q = q_ref[0, :, slabs[h]]
        q_heads.append(jnp.where((lane // SB_HEAD_DIM) == h % 2, q, jnp.zeros_like(q)))

    def logits(start, tk, h):
        start = pl.multiple_of(start, tk)
        return _dot_nt(k_ref[0, pl.ds(start, tk), slabs[h]], q_heads[h])

    def softplus_parts(z, mask):
        sp = jnp.maximum(z, 0.0) + jnp.log(1.0 + jnp.exp2(jnp.abs(z) * (-math.log2(math.e))))
        sp_m = sp if mask is None else jnp.where(mask, sp, 0.0)
        hi = sp_m.astype(jnp.bfloat16)
        lo = (sp_m - hi.astype(jnp.float32)).astype(jnp.bfloat16)
        return sp, jnp.concatenate([hi, lo], axis=0), jnp.sum(sp_m, axis=0, keepdims=True)

    def weights(z, sp, inner, tail, mask):
        w = jnp.exp(z - sp - inner - tail)
        return (w if mask is None else jnp.where(mask, w, 0.0)).astype(jnp.bfloat16)

    def values(v_t, a, valid=None):
        if valid is not None:
            v_t = jnp.where(valid, v_t, jnp.zeros_like(v_t))
        return _dot(v_t, a)

    def head_rows(h):
        return slice(h * SB_HEAD_DIM, (h + 1) * SB_HEAD_DIM)

    b1 = jnp.maximum(qi - 1, 0)
    b2 = jnp.maximum(qi - 2, 0)
    spans = [(qi * blk, blk, qi, slice(0, blk), causal, None),
             (b1 * blk, blk, b1, slice(0, blk), None, qi >= 1),
             (b2 * blk + half, half, b2, slice(half, blk), None, qi >= 2)]
    tiles = [(d, h) for d in range(len(spans)) for h in heads]
    z = {(d, h): logits(spans[d][0], spans[d][1], h) for d, h in tiles}
    parts = {t: softplus_parts(z[t], spans[t[0]][4]) for t in tiles}
    inner = {t: _dot(cum[spans[t[0]][1]], parts[t][1]) for t in tiles}
    tails = [jnp.zeros((1, blk), jnp.float32) for _ in heads]
    a = {}
    for d, h in tiles:
        valid = spans[d][5]
        a[d, h] = weights(z[d, h], parts[d, h][0], inner[d, h], tails[h], spans[d][4])
        total = parts[d, h][2]
        tails[h] = tails[h] + (total if valid is None else jnp.where(valid, total, 0.0))
    accs = [sum(values(v_ref[0, chunk, head_rows(h), lanes], a[d, h], valid)
                for d, (_, _, chunk, lanes, _, valid) in enumerate(spans)) for h in heads]

    def live(state):
        smallest = functools.reduce(jnp.minimum, [state[1 + 2 * h] for h in heads])
        return jnp.logical_and(state[0] >= 0, jnp.min(smallest) < SB_TAIL_CUTOFF)

    def older(state):
        hb = state[0]
        newer_half = hb % 2 == 1
        out = [hb - 1]
        for h in heads:
            tail, acc = state[1 + 2 * h], state[2 + 2 * h]
            zz = logits(hb * half, half, h)
            sp, hilo, total = softplus_parts(zz, None)
            v_blk = v_ref[0, hb // 2, head_rows(h), :]
            v_t = jnp.where(newer_half, v_blk[:, half:], v_blk[:, :half])
            acc = acc + values(v_t, weights(zz, sp, _dot(cum[half], hilo), tail, None))
            out += [tail + total, acc]
        return tuple(out)

    init = [2 * qi - 4]
    for h in heads:
        init += [tails[h], accs[h]]
    state = lax.while_loop(live, older, tuple(init))
    o_t = jnp.concatenate([state[2 + 2 * h] for h in heads], axis=0)
    o_ref[0] = o_t.T.astype(o_ref.dtype)


def _sb_attention(qk, v_t, batch, seq):
    groups = SB_HEADS // SB_HEADS_PER_STEP
    width = SB_HEADS_PER_STEP * SB_HEAD_DIM
    nq = seq // SB_BLOCK
    return pl.pallas_call(
        _sb_attn_kernel,
        grid=(batch, groups, nq),
        in_specs=[
            pl.BlockSpec((1, SB_BLOCK, width), lambda b, p, i: (b, i, p)),
            pl.BlockSpec((1, seq, width), lambda b, p, i: (b, 0, groups + p),
                         pipeline_mode=pl.Buffered(1)),
            pl.BlockSpec((1, nq, width, SB_BLOCK), lambda b, p, i: (b, 0, p, 0),
                         pipeline_mode=pl.Buffered(1)),
        ],
        out_specs=pl.BlockSpec((1, SB_BLOCK, width), lambda b, p, i: (b, i, p)),
        out_shape=jax.ShapeDtypeStruct((batch, seq, SB_HEADS * SB_HEAD_DIM), jnp.bfloat16),
        compiler_params=_params(("parallel", "parallel", "arbitrary")),
        name="sb_attn",
    )(qk, qk, v_t)


def _mla_proj_kernel(x_ref, pos_ref, freq_ref, g_ref, win_ref, qn_ref, kvn_ref,
                     wuq_ref, wuk_ref, wuv_ref, q_ref, k_ref, v_ref):
    h = _rms(x_ref[...], g_ref[...]).astype(jnp.bfloat16)
    proj = _dot(h, win_ref[...])
    c_q = proj[:, :MLA_Q_LORA]
    c_kv = proj[:, MLA_Q_LORA:MLA_Q_LORA + MLA_KV_LORA]
    kr = proj[:, MLA_Q_LORA + MLA_KV_LORA:MLA_Q_LORA + MLA_KV_LORA + HEAD_SLAB]
    kr_rot = proj[:, MLA_Q_LORA + MLA_KV_LORA + HEAD_SLAB:]

    tm = x_ref.shape[0]
    lane = lax.broadcasted_iota(jnp.int32, (tm, HEAD_SLAB), 1)
    ang = pos_ref[...].astype(jnp.float32) * freq_ref[...]
    is_rope = jnp.logical_and(lane >= MLA_NOPE_DIM, lane < MLA_NOPE_DIM + MLA_ROPE_DIM)
    cos = jnp.where(lane < MLA_NOPE_DIM, 1.0, jnp.where(is_rope, jnp.cos(ang), 0.0))
    sin = jnp.where(is_rope, jnp.sin(ang), 0.0)

    qn = _rms(c_q, qn_ref[...]).astype(jnp.bfloat16)
    q_all = _dot(qn, wuq_ref[...])
    kvn = _rms(c_kv, kvn_ref[...]).astype(jnp.bfloat16)
    k_nope = _dot(kvn, wuk_ref[...])
    v_t = _dot_nt(wuv_ref[...], kvn).astype(v_ref.dtype)
    for ci in range(v_ref.shape[0]):
        v_ref[ci] = v_t[:, ci * MLA_V_CHUNK:(ci + 1) * MLA_V_CHUNK]
    k_rope = kr * cos + kr_rot * sin
    cos_q = cos * MLA_SCORE_SCALE
    sin_q = sin * MLA_SCORE_SCALE
    width = MLA_HEADS * HEAD_SLAB
    for hd in range(MLA_HEADS):
        sl = slice(hd * HEAD_SLAB, (hd + 1) * HEAD_SLAB)
        rot = slice(width + hd * HEAD_SLAB, width + (hd + 1) * HEAD_SLAB)
        q_ref[:, sl] = (q_all[:, sl] * cos_q + q_all[:, rot] * sin_q).astype(q_ref.dtype)
        k_ref[:, sl] = (k_nope[:, sl] + k_rope).astype(k_ref.dtype)


def _mla_weights(w_in, w_uq, w_ukv):
    half = MLA_ROPE_DIM // 2
    pad = HEAD_SLAB - MLA_NOPE_DIM - MLA_ROPE_DIM

    def slab(nope, rope):
        return jnp.concatenate(
            [nope, rope, jnp.zeros(rope.shape[:-1] + (pad,), rope.dtype)], axis=-1)

    def rotate(rope):
        return jnp.concatenate([-rope[..., half:], rope[..., :half]], axis=-1)

    lat = MLA_Q_LORA + MLA_KV_LORA
    kr_w = w_in[:, lat:]
    zeros_in = jnp.zeros((D_MODEL, MLA_NOPE_DIM), w_in.dtype)
    w_in_p = jnp.concatenate(
        [w_in[:, :lat], slab(zeros_in, kr_w), slab(zeros_in, rotate(kr_w))], axis=1)

    uq = w_uq.reshape(MLA_Q_LORA, MLA_HEADS, MLA_NOPE_DIM + MLA_ROPE_DIM)
    uq_nope, uq_rope = uq[..., :MLA_NOPE_DIM], uq[..., MLA_NOPE_DIM:]
    w_uq_p = jnp.concatenate(
        [slab(uq_nope, uq_rope).reshape(MLA_Q_LORA, -1),
         slab(jnp.zeros_like(uq_nope), rotate(uq_rope)).reshape(MLA_Q_LORA, -1)], axis=1)

    ukv = w_ukv.reshape(MLA_KV_LORA, MLA_HEADS, MLA_NOPE_DIM + MLA_V_DIM)
    uk = ukv[..., :MLA_NOPE_DIM]
    w_uk_p = jnp.concatenate([uk, jnp.zeros_like(uk)], axis=-1).reshape(MLA_KV_LORA, -1)
    w_uv = ukv[..., MLA_NOPE_DIM:].reshape(MLA_KV_LORA, -1).T
    bf = jnp.bfloat16
    return w_in_p.astype(bf), w_uq_p.astype(bf), w_uk_p.astype(bf), w_uv.astype(bf)


def _rope_freq_slab():
    inv_freq = ROPE_THETA ** (-jnp.arange(0, MLA_ROPE_DIM, 2, dtype=jnp.float32) / MLA_ROPE_DIM)
    zeros = jnp.zeros((MLA_NOPE_DIM,), jnp.float32)
    pad = jnp.zeros((HEAD_SLAB - MLA_NOPE_DIM - MLA_ROPE_DIM,), jnp.float32)
    return jnp.concatenate([zeros, inv_freq, inv_freq, pad])[None, :]


def _mla_proj(x, pos, g, w_in_p, q_norm, kv_norm, w_uq_p, w_uk_p, w_uv):
    tm = MLA_PROJ_TOKENS
    chunks = tm // MLA_V_CHUNK
    t = x.shape[0]
    width = MLA_HEADS * HEAD_SLAB
    vwidth = MLA_HEADS * MLA_V_DIM
    row = lambda i: (i, 0)
    return pl.pallas_call(
        _mla_proj_kernel,
        grid=(t // tm,),
        in_specs=[
            pl.BlockSpec((tm, D_MODEL), row),
            pl.BlockSpec((tm, 1), row),
            _const_spec((1, HEAD_SLAB)),
            _const_spec((1, D_MODEL)),
            _const_spec(w_in_p.shape),
            _const_spec((1, MLA_Q_LORA)),
            _const_spec((1, MLA_KV_LORA)),
            _const_spec(w_uq_p.shape),
            _const_spec(w_uk_p.shape),
            _const_spec(w_uv.shape),
        ],
        out_specs=[
            pl.BlockSpec((tm, width), row),
            pl.BlockSpec((tm, width), row),
            pl.BlockSpec((chunks, vwidth, MLA_V_CHUNK), lambda i: (i, 0, 0)),
        ],
        out_shape=[
            jax.ShapeDtypeStruct((t, width), jnp.bfloat16),
            jax.ShapeDtypeStruct((t, width), jnp.bfloat16),
            jax.ShapeDtypeStruct((t // MLA_V_CHUNK, vwidth, MLA_V_CHUNK), jnp.bfloat16),
        ],
        compiler_params=_params(("parallel",)),
        name="mla_proj",
    )(x, pos, _rope_freq_slab(), g, w_in_p, q_norm, kv_norm, w_uq_p, w_uk_p, w_uv)


def _mla_attn_kernel(q_ref, k_ref, v_ref, o_ref,
                     s0_ref, s1_ref, p0_ref, p1_ref, mx_ref, m_ref, alpha_ref, acc_ref):
    qi = pl.program_id(2)
    tq, tk = MLA_Q_BLOCK, MLA_K_BLOCK
    chunks = tk // MLA_V_CHUNK
    heads = range(HEADS_PER_STEP)
    q_heads = [q_ref[0, :, h * HEAD_SLAB:(h + 1) * HEAD_SLAB] for h in heads]
    ones = jnp.ones((MLA_DENOM_ROWS, tk), jnp.bfloat16)
    ratio = tq // tk
    n_full = qi * ratio
    s_bufs = (s0_ref, s1_ref)
    p_bufs = (p0_ref, p1_ref)

    def scores(j, parity, first=0):
        start = pl.multiple_of(j * tk, tk)
        for h in heads:
            k_blk = k_ref[0, pl.ds(start, tk), h * HEAD_SLAB:(h + 1) * HEAD_SLAB]
            s = _dot_nt(k_blk, q_heads[h][first:])
            s_bufs[parity][h, :, first:tq] = s
            mx_ref[parity, h, :, first:] = jnp.max(s, axis=0, keepdims=True)

    def values(j, parity, first=0):
        for h in heads:
            v_t = jnp.concatenate(
                [v_ref[0, j * chunks + ci, h * MLA_V_DIM:(h + 1) * MLA_V_DIM, :]
                 for ci in range(chunks)], axis=1)
            pv = _dot(jnp.concatenate([v_t, ones], axis=0), p_bufs[parity][h, :, first:tq])
            acc_ref[h, :, first:] = alpha_ref[parity, h, :, first:] * acc_ref[h, :, first:] + pv

    def softmax(parity, mask, first=0, last=tq):
        cols = slice(first, last)
        for h in heads:
            s = s_bufs[parity][h, :, cols]
            if mask is None:
                block_max = mx_ref[parity, h, :, cols]
            else:
                s = jnp.where(mask, s, -jnp.inf)
                block_max = jnp.max(s, axis=0, keepdims=True)
            m_old = m_ref[h, :, cols]
            m_new = jnp.maximum(m_old, block_max)
            p_bufs[parity][h, :, cols] = jnp.exp2(s - m_new).astype(jnp.bfloat16)
            alpha_ref[parity, h, :, cols] = jnp.exp2(m_old - m_new)
            m_ref[h, :, cols] = m_new

    def step(i, parity):
        softmax(parity, None)
        scores(i + 1, 1 - parity)
        values(jnp.maximum(i - 1, 0), 1 - parity)

    def diagonal_steps():
        key = lax.broadcasted_iota(jnp.int32, (tk, tk), 0)
        qry = lax.broadcasted_iota(jnp.int32, (tk, tk), 1)
        for off in range(ratio):
            if off + 1 < ratio:
                scores(n_full + off + 1, (off + 1) % 2, (off + 1) * tk)
            softmax(off % 2, key <= qry, off * tk, (off + 1) * tk)
            if off + 1 < ratio:
                softmax(off % 2, None, (off + 1) * tk)
            values(jnp.maximum(n_full + off - 1, 0), (off + 1) % 2, max(off - 1, 0) * tk)
        values(n_full + ratio - 1, (ratio - 1) % 2, (ratio - 1) * tk)
        o_t = jnp.concatenate(
            [acc_ref[h, :MLA_V_DIM] / acc_ref[h, MLA_V_DIM:MLA_V_DIM + 1] for h in heads], axis=0)
        o_ref[0] = o_t.T.astype(o_ref.dtype)

    m_ref[...] = jnp.full(m_ref.shape, -jnp.inf, jnp.float32)
    alpha_ref[...] = jnp.ones(alpha_ref.shape, jnp.float32)
    acc_ref[...] = jnp.zeros(acc_ref.shape, jnp.float32)

    p1_ref[...] = jnp.zeros(p1_ref.shape, jnp.bfloat16)
    scores(0, 0)

    def two_steps(ii, carry):
        step(2 * ii, 0)
        step(2 * ii + 1, 1)
        return carry

    lax.fori_loop(0, n_full // 2, two_steps, 0)
    diagonal_steps()


def _mla_attention(q, k, v, batch, seq):
    assert MLA_Q_BLOCK % (2 * MLA_K_BLOCK) == 0
    assert seq % MLA_Q_BLOCK == 0 and MLA_K_BLOCK % MLA_V_CHUNK == 0
    pairs = MLA_HEADS // HEADS_PER_STEP
    nq = seq // MLA_Q_BLOCK
    qk_w = HEADS_PER_STEP * HEAD_SLAB
    return pl.pallas_call(
        _mla_attn_kernel,
        grid=(batch, pairs, nq),
        in_specs=[
            pl.BlockSpec((1, MLA_Q_BLOCK, qk_w), lambda b, p, i: (b, i, p)),
            pl.BlockSpec((1, seq, qk_w), lambda b, p, i: (b, 0, p), pipeline_mode=pl.Buffered(1)),
            pl.BlockSpec((1, seq // MLA_V_CHUNK, HEADS_PER_STEP * MLA_V_DIM, MLA_V_CHUNK),
                         lambda b, p, i: (b, 0, p, 0), pipeline_mode=pl.Buffered(1)),
        ],
        out_specs=pl.BlockSpec((1, MLA_Q_BLOCK, HEADS_PER_STEP * MLA_V_DIM),
                               lambda b, p, i: (b, i, p)),
        out_shape=jax.ShapeDtypeStruct((batch, seq, MLA_HEADS * MLA_V_DIM), jnp.bfloat16),
        scratch_shapes=[
            pltpu.VMEM((HEADS_PER_STEP, MLA_K_BLOCK, MLA_Q_BLOCK), jnp.float32),
            pltpu.VMEM((HEADS_PER_STEP, MLA_K_BLOCK, MLA_Q_BLOCK), jnp.float32),
            pltpu.VMEM((HEADS_PER_STEP, MLA_K_BLOCK, MLA_Q_BLOCK), jnp.bfloat16),
            pltpu.VMEM((HEADS_PER_STEP, MLA_K_BLOCK, MLA_Q_BLOCK), jnp.bfloat16),
            pltpu.VMEM((2, HEADS_PER_STEP, 1, MLA_Q_BLOCK), jnp.float32),
            pltpu.VMEM((HEADS_PER_STEP, 1, MLA_Q_BLOCK), jnp.float32),
            pltpu.VMEM((2, HEADS_PER_STEP, 1, MLA_Q_BLOCK), jnp.float32),
            pltpu.VMEM((HEADS_PER_STEP, MLA_V_DIM + MLA_DENOM_ROWS, MLA_Q_BLOCK), jnp.float32),
        ],
        compiler_params=_params(("parallel", "parallel", "arbitrary")),
        name="mla_attn",
    )(q, k, v)


def kernel(x, positions, norm_g, ffn_w_in, ffn_w_out, sb_w_in, sb_w_out, mla_w_in,
           mla_q_norm, mla_w_uq, mla_kv_norm, mla_w_ukv, mla_w_out):
    batch, seq, _ = x.shape
    t = batch * seq
    bf = jnp.bfloat16
    depth = norm_g.shape[0]
    xt = x.reshape(t, D_MODEL)
    pos = positions.reshape(t, 1)
    ffn_in = ffn_w_in.astype(bf)
    ffn_out = ffn_w_out.astype(bf)
    for i in range(depth):
        g = norm_g[i][:, None, :]
        xt = _ffn(xt, g[0], g[1], ffn_in, ffn_out, i, 0)
        j = i // 2
        if i % 2 == 0:
            width = SB_HEADS * SB_HEAD_DIM
            w_qk = sb_w_in[j][:, :2 * width].astype(bf)
            w_vt = sb_w_in[j][:, 2 * width:].T.astype(bf)
            qk, v_t = _sb_qkv(xt, g[2], w_qk, w_vt)
            o = _sb_attention(qk.reshape(batch, seq, -1),
                              v_t.reshape(batch, seq // SB_BLOCK, width, SB_BLOCK), batch, seq)
            w_out = sb_w_out[j]
        else:
            w_in_p, w_uq_p, w_uk_p, w_uv = _mla_weights(mla_w_in[j], mla_w_uq[j], mla_w_ukv[j])
            q, k, v = _mla_proj(xt, pos, g[2], w_in_p, mla_q_norm[j][None, :],
                                mla_kv_norm[j][None, :], w_uq_p, w_uk_p, w_uv)
            v_t = v.reshape(batch, seq // MLA_V_CHUNK, MLA_HEADS * MLA_V_DIM, MLA_V_CHUNK)
            o = _mla_attention(q.reshape(batch, seq, -1), k.reshape(batch, seq, -1),
                               v_t, batch, seq)
            w_out = mla_w_out[j]
        xt = _mixer_out_ffn(xt, o.reshape(t, -1), w_out.astype(bf), g[3], g[4], g[5],
                            ffn_in, ffn_out, i, 1)
    return xt.reshape(batch, seq, D_MODEL)
```

```python
import functools
import math

import jax
import jax.numpy as jnp
from jax import lax
from jax.experimental import pallas as pl
from jax.experimental.pallas import tpu as pltpu

D_MODEL = 1024
EPS = 1e-6
D_FF = 2816
FFN_RESIDUAL_WEIGHT = 0.5

SB_HEADS = 16
SB_HEAD_DIM = 64

MLA_HEADS = 16
MLA_Q_LORA = 256
MLA_KV_LORA = 128
MLA_NOPE_DIM = 64
MLA_ROPE_DIM = 32
MLA_V_DIM = 64
ROPE_THETA = 10000.0

LANES = 128
HEAD_SLAB = 128
HEADS_PER_STEP = 4
VMEM_LIMIT_BYTES = 56 * 1024 * 1024

SB_TAIL_CUTOFF = 104.0
SB_BLOCK = 128
SB_HEADS_PER_STEP = 16

MLA_Q_BLOCK = 1024
MLA_K_BLOCK = 512
MLA_PROJ_TOKENS = 512
MLA_V_CHUNK = 256
MLA_SCORE_SCALE = math.log2(math.e) / math.sqrt(MLA_NOPE_DIM + MLA_ROPE_DIM)
MLA_DENOM_ROWS = 16


def _rms(x, g):
    return x * lax.rsqrt(jnp.mean(x * x, axis=-1, keepdims=True) + EPS) * g


def _dot(a, b):
    return jnp.dot(a, b, preferred_element_type=jnp.float32)


def _dot_nt(a, b):
    return lax.dot_general(a, b, (((1,), (1,)), ((), ())),
                           preferred_element_type=jnp.float32)


def _params(semantics):
    return pltpu.CompilerParams(dimension_semantics=semantics,
                                vmem_limit_bytes=VMEM_LIMIT_BYTES)


def _const_spec(shape):
    return pl.BlockSpec(shape, lambda *_: (0,) * len(shape),
                        pipeline_mode=pl.Buffered(1))


FFN_TOKENS = 1024
MIXER_FFN_TOKENS = 512
FFN_SUB_TOKENS = 256


def _ffn_block(xs, gpre_ref, gpost_ref, win_ref, wout_ref):
    xn = [_rms(x, gpre_ref[...]).astype(jnp.bfloat16) for x in xs]
    h = [_dot(v, win_ref[...]) for v in xn]
    act = [(v[:, :D_FF] * jax.nn.sigmoid(v[:, :D_FF]) * v[:, D_FF:]).astype(jnp.bfloat16)
           for v in h]
    f = [_dot(v, wout_ref[...]) for v in act]
    return jnp.concatenate(
        [x + FFN_RESIDUAL_WEIGHT * _rms(fx, gpost_ref[...]) for x, fx in zip(xs, f)], axis=0)


def _sub_tiles(ref):
    return [ref[r:r + FFN_SUB_TOKENS, :] for r in range(0, ref.shape[0], FFN_SUB_TOKENS)]


def _ffn_kernel(x_ref, gpre_ref, gpost_ref, win_ref, wout_ref, y_ref):
    y_ref[...] = _ffn_block(_sub_tiles(x_ref), gpre_ref, gpost_ref, win_ref, wout_ref)


def _mixer_out_ffn_kernel(x_ref, o_ref, wo_ref, gmix_ref, gpre_ref, gpost_ref, win_ref, wout_ref,
                          y_ref):
    mixed = [_dot(o, wo_ref[...]) for o in _sub_tiles(o_ref)]
    xs = [x + _rms(m, gmix_ref[...]) for x, m in zip(_sub_tiles(x_ref), mixed)]
    y_ref[...] = _ffn_block(xs, gpre_ref, gpost_ref, win_ref, wout_ref)


def _ffn_specs(layer, half):
    def weight(rows, cols):
        return pl.BlockSpec((None, None, rows, cols), lambda *_: (layer, half, 0, 0),
                            pipeline_mode=pl.Buffered(1))
    return [
        _const_spec((1, D_MODEL)),
        _const_spec((1, D_MODEL)),
        weight(D_MODEL, 2 * D_FF),
        weight(D_FF, D_MODEL),
    ]


def _ffn(x, g_pre, g_post, w_in, w_out, layer, half):
    t = x.shape[0]
    rows = pl.BlockSpec((FFN_TOKENS, D_MODEL), lambda i: (i, 0))
    return pl.pallas_call(
        _ffn_kernel,
        grid=(t // FFN_TOKENS,),
        in_specs=[rows] + _ffn_specs(layer, half),
        out_specs=rows,
        out_shape=jax.ShapeDtypeStruct((t, D_MODEL), jnp.float32),
        compiler_params=_params(("parallel",)),
        name="ffn",
    )(x, g_pre, g_post, w_in, w_out)


def _mixer_out_ffn(x, o, w_o, g_mix, g_pre, g_post, w_in, w_out, layer, half):
    t = x.shape[0]
    rows = pl.BlockSpec((MIXER_FFN_TOKENS, D_MODEL), lambda i: (i, 0))
    return pl.pallas_call(
        _mixer_out_ffn_kernel,
        grid=(t // MIXER_FFN_TOKENS,),
        in_specs=[rows, rows, _const_spec((D_MODEL, D_MODEL)), _const_spec((1, D_MODEL))]
        + _ffn_specs(layer, half),
        out_specs=rows,
        out_shape=jax.ShapeDtypeStruct((t, D_MODEL), jnp.float32),
        compiler_params=_params(("parallel",)),
        name="mixer_out_ffn",
    )(x, o, w_o, g_mix, g_pre, g_post, w_in, w_out)


def _sb_qkv_kernel(x_ref, g_ref, wqk_ref, wvt_ref, qk_ref, vt_ref):
    h = _rms(x_ref[...], g_ref[...]).astype(jnp.bfloat16)
    width = SB_HEADS * SB_HEAD_DIM
    y = _dot(h, wqk_ref[...])
    qk_ref[:, :width] = (y[:, :width] * (1.0 / math.sqrt(SB_HEAD_DIM))).astype(qk_ref.dtype)
    qk_ref[:, width:] = y[:, width:].astype(qk_ref.dtype)
    v_t = _dot_nt(wvt_ref[...], h).astype(vt_ref.dtype)
    for ci in range(vt_ref.shape[0]):
        vt_ref[ci] = v_t[:, ci * SB_BLOCK:(ci + 1) * SB_BLOCK]


def _sb_qkv(x, g, w_qk, w_vt, tm=512):
    t = x.shape[0]
    width = SB_HEADS * SB_HEAD_DIM
    chunks = tm // SB_BLOCK
    return pl.pallas_call(
        _sb_qkv_kernel,
        grid=(t // tm,),
        in_specs=[
            pl.BlockSpec((tm, D_MODEL), lambda i: (i, 0)),
            _const_spec((1, D_MODEL)),
            _const_spec((D_MODEL, 2 * width)),
            _const_spec((width, D_MODEL)),
        ],
        out_specs=[
            pl.BlockSpec((tm, 2 * width), lambda i: (i, 0)),
            pl.BlockSpec((chunks, width, SB_BLOCK), lambda i: (i, 0, 0)),
        ],
        out_shape=[
            jax.ShapeDtypeStruct((t, 2 * width), jnp.bfloat16),
            jax.ShapeDtypeStruct((t // SB_BLOCK, width, SB_BLOCK), jnp.bfloat16),
        ],
        compiler_params=_params(("parallel",)),
        name="sb_qkv",
    )(x, g, w_qk, w_vt)


def _sb_attn_kernel(q_ref, k_ref, v_ref, o_ref):
    qi = pl.program_id(2)
    blk = SB_BLOCK
    heads = range(SB_HEADS_PER_STEP)
    lane = lax.broadcasted_iota(jnp.int32, (blk, LANES), 1)
    key = lax.broadcasted_iota(jnp.int32, (blk, blk), 0)
    qry = lax.broadcasted_iota(jnp.int32, (blk, blk), 1)
    causal = key < qry
    half = blk // 2

    def cum_matrix(tk):
        ur = lax.broadcasted_iota(jnp.int32, (tk, 2 * tk), 0)
        uc = lax.broadcasted_iota(jnp.int32, (tk, 2 * tk), 1)
        return jnp.where((uc % tk) > ur, 1.0, 0.0).astype(jnp.bfloat16)

    cum = {blk: cum_matrix(blk), half: cum_matrix(half)}
    slabs = [slice((h // 2) * LANES, (h // 2 + 1) * LANES) for h in heads]
    q_heads = []
    for h in heads:
        q = q_ref[0, :, slabs[h]]
        q_heads.append(jnp.where((lane // SB_HEAD_DIM) == h % 2, q, jnp.zeros_like(q)))

    def logits(start, tk, h):
        start = pl.multiple_of(start, tk)
        return _dot_nt(k_ref[0, pl.ds(start, tk), slabs[h]], q_heads[h])

    def softplus_parts(z, mask):
        sp = jnp.maximum(z, 0.0) + jnp.log(1.0 + jnp.exp2(jnp.abs(z) * (-math.log2(math.e))))
        sp_m = sp if mask is None else jnp.where(mask, sp, 0.0)
        hi = sp_m.astype(jnp.bfloat16)
        lo = (sp_m - hi.astype(jnp.float32)).astype(jnp.bfloat16)
        return sp, jnp.concatenate([hi, lo], axis=0), jnp.sum(sp_m, axis=0, keepdims=True)

    def weights(z, sp, inner, tail, mask):
        w = jnp.exp(z - sp - inner - tail)
        return (w if mask is None else jnp.where(mask, w, 0.0)).astype(jnp.bfloat16)

    def values(v_t, a, valid=None):
        if valid is not None:
            v_t = jnp.where(valid, v_t, jnp.zeros_like(v_t))
        return _dot(v_t, a)

    def head_rows(h):
        return slice(h * SB_HEAD_DIM, (h + 1) * SB_HEAD_DIM)

    b1 = jnp.maximum(qi - 1, 0)
    b2 = jnp.maximum(qi - 2, 0)
    spans = [(qi * blk, blk, qi, slice(0, blk), causal, None),
             (b1 * blk, blk, b1, slice(0, blk), None, qi >= 1),
             (b2 * blk + half, half, b2, slice(half, blk), None, qi >= 2)]
    tiles = [(d, h) for d in range(len(spans)) for h in heads]
    z = {(d, h): logits(spans[d][0], spans[d][1], h) for d, h in tiles}
    parts = {t: softplus_parts(z[t], spans[t[0]][4]) for t in tiles}
    inner = {t: _dot(cum[spans[t[0]][1]], parts[t][1]) for t in tiles}
    tails = [jnp.zeros((1, blk), jnp.float32) for _ in heads]
    a = {}
    for d, h in tiles:
        valid = spans[d][5]
        a[d, h] = weights(z[d, h], parts[d, h][0], inner[d, h], tails[h], spans[d][4])
        total = parts[d, h][2]
        tails[h] = tails[h] + (total if valid is None else jnp.where(valid, total, 0.0))
    accs = [sum(values(v_ref[0, chunk, head_rows(h), lanes], a[d, h], valid)
                for d, (_, _, chunk, lanes, _, valid) in enumerate(spans)) for h in heads]

    def live(state):
        smallest = functools.reduce(jnp.minimum, [state[1 + 2 * h] for h in heads])
        return jnp.logical_and(state[0] >= 0, jnp.min(smallest) < SB_TAIL_CUTOFF)

    def older(state):
        hb = state[0]
        newer_half = hb % 2 == 1
        out = [hb - 1]
        for h in heads:
            tail, acc = state[1 + 2 * h], state[2 + 2 * h]
            zz = logits(hb * half, half, h)
            sp, hilo, total = softplus_parts(zz, None)
            v_blk = v_ref[0, hb // 2, head_rows(h), :]
            v_t = jnp.where(newer_half, v_blk[:, half:], v_blk[:, :half])
            acc = acc + values(v_t, weights(zz, sp, _dot(cum[half], hilo), tail, None))
            out += [tail + total, acc]
        return tuple(out)

    init = [2 * qi - 4]
    for h in heads:
        init += [tails[h], accs[h]]
    state = lax.while_loop(live, older, tuple(init))
    o_t = jnp.concatenate([state[2 + 2 * h] for h in heads], axis=0)
    o_ref[0] = o_t.T.astype(o_ref.dtype)


def _sb_attention(qk, v_t, batch, seq):
    groups = SB_HEADS // SB_HEADS_PER_STEP
    width = SB_HEADS_PER_STEP * SB_HEAD_DIM
    nq = seq // SB_BLOCK
    return pl.pallas_call(
        _sb_attn_kernel,
        grid=(batch, groups, nq),
        in_specs=[
            pl.BlockSpec((1, SB_BLOCK, width), lambda b, p, i: (b, i, p)),
            pl.BlockSpec((1, seq, width), lambda b, p, i: (b, 0, groups + p),
                         pipeline_mode=pl.Buffered(1)),
            pl.BlockSpec((1, nq, width, SB_BLOCK), lambda b, p, i: (b, 0, p, 0),
                         pipeline_mode=pl.Buffered(1)),
        ],
        out_specs=pl.BlockSpec((1, SB_BLOCK, width), lambda b, p, i: (b, i, p)),
        out_shape=jax.ShapeDtypeStruct((batch, seq, SB_HEADS * SB_HEAD_DIM), jnp.bfloat16),
        compiler_params=_params(("parallel", "parallel", "arbitrary")),
        name="sb_attn",
    )(qk, qk, v_t)


def _mla_proj_kernel(x_ref, pos_ref, freq_ref, g_ref, win_ref, qn_ref, kvn_ref,
                     wuq_ref, wuk_ref, wuv_ref, q_ref, k_ref, v_ref):
    h = _rms(x_ref[...], g_ref[...]).astype(jnp.bfloat16)
    proj = _dot(h, win_ref[...])
    c_q = proj[:, :MLA_Q_LORA]
    c_kv = proj[:, MLA_Q_LORA:MLA_Q_LORA + MLA_KV_LORA]
    kr = proj[:, MLA_Q_LORA + MLA_KV_LORA:MLA_Q_LORA + MLA_KV_LORA + HEAD_SLAB]
    kr_rot = proj[:, MLA_Q_LORA + MLA_KV_LORA + HEAD_SLAB:]

    tm = x_ref.shape[0]
    lane = lax.broadcasted_iota(jnp.int32, (tm, HEAD_SLAB), 1)
    ang = pos_ref[...].astype(jnp.float32) * freq_ref[...]
    is_rope = jnp.logical_and(lane >= MLA_NOPE_DIM, lane < MLA_NOPE_DIM + MLA_ROPE_DIM)
    cos = jnp.where(lane < MLA_NOPE_DIM, 1.0, jnp.where(is_rope, jnp.cos(ang), 0.0))
    sin = jnp.where(is_rope, jnp.sin(ang), 0.0)

    qn = _rms(c_q, qn_ref[...]).astype(jnp.bfloat16)
    q_all = _dot(qn, wuq_ref[...])
    kvn = _rms(c_kv, kvn_ref[...]).astype(jnp.bfloat16)
    k_nope = _dot(kvn, wuk_ref[...])
    v_t = _dot_nt(wuv_ref[...], kvn).astype(v_ref.dtype)
    for ci in range(v_ref.shape[0]):
        v_ref[ci] = v_t[:, ci * MLA_V_CHUNK:(ci + 1) * MLA_V_CHUNK]
    k_rope = kr * cos + kr_rot * sin
    cos_q = cos * MLA_SCORE_SCALE
    sin_q = sin * MLA_SCORE_SCALE
    width = MLA_HEADS * HEAD_SLAB
    for hd in range(MLA_HEADS):
        sl = slice(hd * HEAD_SLAB, (hd + 1) * HEAD_SLAB)
        rot = slice(width + hd * HEAD_SLAB, width + (hd + 1) * HEAD_SLAB)
        q_ref[:, sl] = (q_all[:, sl] * cos_q + q_all[:, rot] * sin_q).astype(q_ref.dtype)
        k_ref[:, sl] = (k_nope[:, sl] + k_rope).astype(k_ref.dtype)


def _mla_weights(w_in, w_uq, w_ukv):
    half = MLA_ROPE_DIM // 2
    pad = HEAD_SLAB - MLA_NOPE_DIM - MLA_ROPE_DIM

    def slab(nope, rope):
        return jnp.concatenate(
            [nope, rope, jnp.zeros(rope.shape[:-1] + (pad,), rope.dtype)], axis=-1)

    def rotate(rope):
        return jnp.concatenate([-rope[..., half:], rope[..., :half]], axis=-1)

    lat = MLA_Q_LORA + MLA_KV_LORA
    kr_w = w_in[:, lat:]
    zeros_in = jnp.zeros((D_MODEL, MLA_NOPE_DIM), w_in.dtype)
    w_in_p = jnp.concatenate(
        [w_in[:, :lat], slab(zeros_in, kr_w), slab(zeros_in, rotate(kr_w))], axis=1)

    uq = w_uq.reshape(MLA_Q_LORA, MLA_HEADS, MLA_NOPE_DIM + MLA_ROPE_DIM)
    uq_nope, uq_rope = uq[..., :MLA_NOPE_DIM], uq[..., MLA_NOPE_DIM:]
    w_uq_p = jnp.concatenate(
        [slab(uq_nope, uq_rope).reshape(MLA_Q_LORA, -1),
         slab(jnp.zeros_like(uq_nope), rotate(uq_rope)).reshape(MLA_Q_LORA, -1)], axis=1)

    ukv = w_ukv.reshape(MLA_KV_LORA, MLA_HEADS, MLA_NOPE_DIM + MLA_V_DIM)
    uk = ukv[..., :MLA_NOPE_DIM]
    w_uk_p = jnp.concatenate([uk, jnp.zeros_like(uk)], axis=-1).reshape(MLA_KV_LORA, -1)
    w_uv = ukv[..., MLA_NOPE_DIM:].reshape(MLA_KV_LORA, -1).T
    bf = jnp.bfloat16
    return w_in_p.astype(bf), w_uq_p.astype(bf), w_uk_p.astype(bf), w_uv.astype(bf)


def _rope_freq_slab():
    inv_freq = ROPE_THETA ** (-jnp.arange(0, MLA_ROPE_DIM, 2, dtype=jnp.float32) / MLA_ROPE_DIM)
    zeros = jnp.zeros((MLA_NOPE_DIM,), jnp.float32)
    pad = jnp.zeros((HEAD_SLAB - MLA_NOPE_DIM - MLA_ROPE_DIM,), jnp.float32)
    return jnp.concatenate([zeros, inv_freq, inv_freq, pad])[None, :]


def _mla_proj(x, pos, g, w_in_p, q_norm, kv_norm, w_uq_p, w_uk_p, w_uv):
    tm = MLA_PROJ_TOKENS
    chunks = tm // MLA_V_CHUNK
    t = x.shape[0]
    width = MLA_HEADS * HEAD_SLAB
    vwidth = MLA_HEADS * MLA_V_DIM
    row = lambda i: (i, 0)
    return pl.pallas_call(
        _mla_proj_kernel,
        grid=(t // tm,),
        in_specs=[
            pl.BlockSpec((tm, D_MODEL), row),
            pl.BlockSpec((tm, 1), row),
            _const_spec((1, HEAD_SLAB)),
            _const_spec((1, D_MODEL)),
            _const_spec(w_in_p.shape),
            _const_spec((1, MLA_Q_LORA)),
            _const_spec((1, MLA_KV_LORA)),
            _const_spec(w_uq_p.shape),
            _const_spec(w_uk_p.shape),
            _const_spec(w_uv.shape),
        ],
        out_specs=[
            pl.BlockSpec((tm, width), row),
            pl.BlockSpec((tm, width), row),
            pl.BlockSpec((chunks, vwidth, MLA_V_CHUNK), lambda i: (i, 0, 0)),
        ],
        out_shape=[
            jax.ShapeDtypeStruct((t, width), jnp.bfloat16),
            jax.ShapeDtypeStruct((t, width), jnp.bfloat16),
            jax.ShapeDtypeStruct((t // MLA_V_CHUNK, vwidth, MLA_V_CHUNK), jnp.bfloat16),
        ],
        compiler_params=_params(("parallel",)),
        name="mla_proj",
    )(x, pos, _rope_freq_slab(), g, w_in_p, q_norm, kv_norm, w_uq_p, w_uk_p, w_uv)


def _mla_attn_kernel(q_ref, k_ref, v_ref, o_ref,
                     s0_ref, s1_ref, p0_ref, p1_ref, mx_ref, m_ref, alpha_ref, acc_ref):
    qi = pl.program_id(2)
    tq, tk = MLA_Q_BLOCK, MLA_K_BLOCK
    chunks = tk // MLA_V_CHUNK
    heads = range(HEADS_PER_STEP)
    q_heads = [q_ref[0, :, h * HEAD_SLAB:(h + 1) * HEAD_SLAB] for h in heads]
    ones = jnp.ones((MLA_DENOM_ROWS, tk), jnp.bfloat16)
    ratio = tq // tk
    n_full = qi * ratio
    s_bufs = (s0_ref, s1_ref)
    p_bufs = (p0_ref, p1_ref)

    def scores(j, parity, first=0):
        start = pl.multiple_of(j * tk, tk)
        for h in heads:
            k_blk = k_ref[0, pl.ds(start, tk), h * HEAD_SLAB:(h + 1) * HEAD_SLAB]
            s = _dot_nt(k_blk, q_heads[h][first:])
            s_bufs[parity][h, :, first:tq] = s
            mx_ref[parity, h, :, first:] = jnp.max(s, axis=0, keepdims=True)

    def values(j, parity, first=0):
        for h in heads:
            v_t = jnp.concatenate(
                [v_ref[0, j * chunks + ci, h * MLA_V_DIM:(h + 1) * MLA_V_DIM, :]
                 for ci in range(chunks)], axis=1)
            pv = _dot(jnp.concatenate([v_t, ones], axis=0), p_bufs[parity][h, :, first:tq])
            acc_ref[h, :, first:] = alpha_ref[parity, h, :, first:] * acc_ref[h, :, first:] + pv

    def softmax(parity, mask, first=0, last=tq):
        cols = slice(first, last)
        for h in heads:
            s = s_bufs[parity][h, :, cols]
            if mask is None:
                block_max = mx_ref[parity, h, :, cols]
            else:
                s = jnp.where(mask, s, -jnp.inf)
                block_max = jnp.max(s, axis=0, keepdims=True)
            m_old = m_ref[h, :, cols]
            m_new = jnp.maximum(m_old, block_max)
            p_bufs[parity][h, :, cols] = jnp.exp2(s - m_new).astype(jnp.bfloat16)
            alpha_ref[parity, h, :, cols] = jnp.exp2(m_old - m_new)
            m_ref[h, :, cols] = m_new

    def step(i, parity):
        softmax(parity, None)
        scores(i + 1, 1 - parity)
        values(jnp.maximum(i - 1, 0), 1 - parity)

    def diagonal_steps():
        key = lax.broadcasted_iota(jnp.int32, (tk, tk), 0)
        qry = lax.broadcasted_iota(jnp.int32, (tk, tk), 1)
        for off in range(ratio):
            if off + 1 < ratio:
                scores(n_full + off + 1, (off + 1) % 2, (off + 1) * tk)
            softmax(off % 2, key <= qry, off * tk, (off + 1) * tk)
            if off + 1 < ratio:
                softmax(off % 2, None, (off + 1) * tk)
            values(jnp.maximum(n_full + off - 1, 0), (off + 1) % 2, max(off - 1, 0) * tk)
        values(n_full + ratio - 1, (ratio - 1) % 2, (ratio - 1) * tk)
        o_t = jnp.concatenate(
            [acc_ref[h, :MLA_V_DIM] / acc_ref[h, MLA_V_DIM:MLA_V_DIM + 1] for h in heads], axis=0)
        o_ref[0] = o_t.T.astype(o_ref.dtype)

    m_ref[...] = jnp.full(m_ref.shape, -jnp.inf, jnp.float32)
    alpha_ref[...] = jnp.ones(alpha_ref.shape, jnp.float32)
    acc_ref[...] = jnp.zeros(acc_ref.shape, jnp.float32)

    p1_ref[...] = jnp.zeros(p1_ref.shape, jnp.bfloat16)
    scores(0, 0)

    def two_steps(ii, carry):
        step(2 * ii, 0)
        step(2 * ii + 1, 1)
        return carry

    lax.fori_loop(0, n_full // 2, two_steps, 0)
    diagonal_steps()


def _mla_attention(q, k, v, batch, seq):
    assert MLA_Q_BLOCK % (2 * MLA_K_BLOCK) == 0
    assert seq % MLA_Q_BLOCK == 0 and MLA_K_BLOCK % MLA_V_CHUNK == 0
    pairs = MLA_HEADS // HEADS_PER_STEP
    nq = seq // MLA_Q_BLOCK
    qk_w = HEADS_PER_STEP * HEAD_SLAB
    return pl.pallas_call(
        _mla_attn_kernel,
        grid=(batch, pairs, nq),
        in_specs=[
            pl.BlockSpec((1, MLA_Q_BLOCK, qk_w), lambda b, p, i: (b, i, p)),
            pl.BlockSpec((1, seq, qk_w), lambda b, p, i: (b, 0, p), pipeline_mode=pl.Buffered(1)),
            pl.BlockSpec((1, seq // MLA_V_CHUNK, HEADS_PER_STEP * MLA_V_DIM, MLA_V_CHUNK),
                         lambda b, p, i: (b, 0, p, 0), pipeline_mode=pl.Buffered(1)),
        ],
        out_specs=pl.BlockSpec((1, MLA_Q_BLOCK, HEADS_PER_STEP * MLA_V_DIM),
                               lambda b, p, i: (b, i, p)),
        out_shape=jax.ShapeDtypeStruct((batch, seq, MLA_HEADS * MLA_V_DIM), jnp.bfloat16),
        scratch_shapes=[
            pltpu.VMEM((HEADS_PER_STEP, MLA_K_BLOCK, MLA_Q_BLOCK), jnp.float32),
            pltpu.VMEM((HEADS_PER_STEP, MLA_K_BLOCK, MLA_Q_BLOCK), jnp.float32),
            pltpu.VMEM((HEADS_PER_STEP, MLA_K_BLOCK, MLA_Q_BLOCK), jnp.bfloat16),
            pltpu.VMEM((HEADS_PER_STEP, MLA_K_BLOCK, MLA_Q_BLOCK), jnp.bfloat16),
            pltpu.VMEM((2, HEADS_PER_STEP, 1, MLA_Q_BLOCK), jnp.float32),
            pltpu.VMEM((HEADS_PER_STEP, 1, MLA_Q_BLOCK), jnp.float32),
            pltpu.VMEM((2, HEADS_PER_STEP, 1, MLA_Q_BLOCK), jnp.float32),
            pltpu.VMEM((HEADS_PER_STEP, MLA_V_DIM + MLA_DENOM_ROWS, MLA_Q_BLOCK), jnp.float32),
        ],
        compiler_params=_params(("parallel", "parallel", "arbitrary")),
        name="mla_attn",
    )(q, k, v)


def kernel(x, positions, norm_g, ffn_w_in, ffn_w_out, sb_w_in, sb_w_out, mla_w_in,
           mla_q_norm, mla_w_uq, mla_kv_norm, mla_w_ukv, mla_w_out):
    batch, seq, _ = x.shape
    t = batch * seq
    bf = jnp.bfloat16
    depth = norm_g.shape[0]
    xt = x.reshape(t, D_MODEL)
    pos = positions.reshape(t, 1)
    ffn_in = ffn_w_in.astype(bf)
    ffn_out = ffn_w_out.astype(bf)
    for i in range(depth):
        g = norm_g[i][:, None, :]
        xt = _ffn(xt, g[0], g[1], ffn_in, ffn_out, i, 0)
        j = i // 2
        if i % 2 == 0:
            width = SB_HEADS * SB_HEAD_DIM
            w_qk = sb_w_in[j][:, :2 * width].astype(bf)
            w_vt = sb_w_in[j][:, 2 * width:].T.astype(bf)
            qk, v_t = _sb_qkv(xt, g[2], w_qk, w_vt)
            o = _sb_attention(qk.reshape(batch, seq, -1),
                              v_t.reshape(batch, seq // SB_BLOCK, width, SB_BLOCK), batch, seq)
            w_out = sb_w_out[j]
        else:
            w_in_p, w_uq_p, w_uk_p, w_uv = _mla_weights(mla_w_in[j], mla_w_uq[j], mla_w_ukv[j])
            q, k, v = _mla_proj(xt, pos, g[2], w_in_p, mla_q_norm[j][None, :],
                                mla_kv_norm[j][None, :], w_uq_p, w_uk_p, w_uv)
            v_t = v.reshape(batch, seq // MLA_V_CHUNK, MLA_HEADS * MLA_V_DIM, MLA_V_CHUNK)
            o = _mla_attention(q.reshape(batch, seq, -1), k.reshape(batch, seq, -1),
                               v_t, batch, seq)
            w_out = mla_w_out[j]
        xt = _mixer_out_ffn(xt, o.reshape(t, -1), w_out.astype(bf), g[3], g[4], g[5],
                            ffn_in, ffn_out, i, 1)
    return xt.reshape(batch, seq, D_MODEL)
```

```python
import functools
import math

import jax
import jax.numpy as jnp
from jax import lax
from jax.experimental import pallas as pl
from jax.experimental.pallas import tpu as pltpu

D_MODEL = 1024
EPS = 1e-6
D_FF = 2816
FFN_RESIDUAL_WEIGHT = 0.5

SB_HEADS = 16
SB_HEAD_DIM = 64

MLA_HEADS = 16
MLA_Q_LORA = 256
MLA_KV_LORA = 128
MLA_NOPE_DIM = 64
MLA_ROPE_DIM = 32
MLA_V_DIM = 64
ROPE_THETA = 10000.0

LANES = 128
HEAD_SLAB = 128
HEADS_PER_STEP = 4
VMEM_LIMIT_BYTES = 56 * 1024 * 1024

SB_TAIL_CUTOFF = 104.0
SB_BLOCK = 128
SB_TOTAL_ROWS = 16
SB_HEADS_PER_STEP = 16

MLA_Q_BLOCK = 1024
MLA_K_BLOCK = 512
MLA_PROJ_TOKENS = 512
MLA_V_CHUNK = 256
MLA_SCORE_SCALE = math.log2(math.e) / math.sqrt(MLA_NOPE_DIM + MLA_ROPE_DIM)
MLA_DENOM_ROWS = 16


def _rms(x, g):
    return x * lax.rsqrt(jnp.mean(x * x, axis=-1, keepdims=True) + EPS) * g


def _dot(a, b):
    return jnp.dot(a, b, preferred_element_type=jnp.float32)


def _dot_nt(a, b):
    return lax.dot_general(a, b, (((1,), (1,)), ((), ())),
                           preferred_element_type=jnp.float32)


def _params(semantics):
    return pltpu.CompilerParams(dimension_semantics=semantics,
                                vmem_limit_bytes=VMEM_LIMIT_BYTES)


def _const_spec(shape):
    return pl.BlockSpec(shape, lambda *_: (0,) * len(shape),
                        pipeline_mode=pl.Buffered(1))


FFN_TOKENS = 1024
MIXER_FFN_TOKENS = 512
FFN_SUB_TOKENS = 256


def _ffn_block(xs, gpre_ref, gpost_ref, win_ref, wout_ref):
    xn = [_rms(x, gpre_ref[...]).astype(jnp.bfloat16) for x in xs]
    h = [_dot(v, win_ref[...]) for v in xn]
    act = [(v[:, :D_FF] * jax.nn.sigmoid(v[:, :D_FF]) * v[:, D_FF:]).astype(jnp.bfloat16)
           for v in h]
    f = [_dot(v, wout_ref[...]) for v in act]
    return jnp.concatenate(
        [x + FFN_RESIDUAL_WEIGHT * _rms(fx, gpost_ref[...]) for x, fx in zip(xs, f)], axis=0)


def _sub_tiles(ref):
    return [ref[r:r + FFN_SUB_TOKENS, :] for r in range(0, ref.shape[0], FFN_SUB_TOKENS)]


def _ffn_kernel(x_ref, gpre_ref, gpost_ref, win_ref, wout_ref, y_ref):
    y_ref[...] = _ffn_block(_sub_tiles(x_ref), gpre_ref, gpost_ref, win_ref, wout_ref)


def _mixer_out_ffn_kernel(x_ref, o_ref, wo_ref, gmix_ref, gpre_ref, gpost_ref, win_ref, wout_ref,
                          y_ref):
    mixed = [_dot(o, wo_ref[...]) for o in _sub_tiles(o_ref)]
    xs = [x + _rms(m, gmix_ref[...]) for x, m in zip(_sub_tiles(x_ref), mixed)]
    y_ref[...] = _ffn_block(xs, gpre_ref, gpost_ref, win_ref, wout_ref)


def _ffn_specs(layer, half):
    def weight(rows, cols):
        return pl.BlockSpec((None, None, rows, cols), lambda *_: (layer, half, 0, 0),
                            pipeline_mode=pl.Buffered(1))
    return [
        _const_spec((1, D_MODEL)),
        _const_spec((1, D_MODEL)),
        weight(D_MODEL, 2 * D_FF),
        weight(D_FF, D_MODEL),
    ]


def _ffn(x, g_pre, g_post, w_in, w_out, layer, half):
    t = x.shape[0]
    rows = pl.BlockSpec((FFN_TOKENS, D_MODEL), lambda i: (i, 0))
    return pl.pallas_call(
        _ffn_kernel,
        grid=(t // FFN_TOKENS,),
        in_specs=[rows] + _ffn_specs(layer, half),
        out_specs=rows,
        out_shape=jax.ShapeDtypeStruct((t, D_MODEL), jnp.float32),
        compiler_params=_params(("parallel",)),
        name="ffn",
    )(x, g_pre, g_post, w_in, w_out)


def _mixer_out_ffn(x, o, w_o, g_mix, g_pre, g_post, w_in, w_out, layer, half):
    t = x.shape[0]
    rows = pl.BlockSpec((MIXER_FFN_TOKENS, D_MODEL), lambda i: (i, 0))
    return pl.pallas_call(
        _mixer_out_ffn_kernel,
        grid=(t // MIXER_FFN_TOKENS,),
        in_specs=[rows, rows, _const_spec((D_MODEL, D_MODEL)), _const_spec((1, D_MODEL))]
        + _ffn_specs(layer, half),
        out_specs=rows,
        out_shape=jax.ShapeDtypeStruct((t, D_MODEL), jnp.float32),
        compiler_params=_params(("parallel",)),
        name="mixer_out_ffn",
    )(x, o, w_o, g_mix, g_pre, g_post, w_in, w_out)


def _sb_qkv_kernel(x_ref, g_ref, wqk_ref, wvt_ref, qk_ref, vt_ref):
    h = _rms(x_ref[...], g_ref[...]).astype(jnp.bfloat16)
    width = SB_HEADS * SB_HEAD_DIM
    y = _dot(h, wqk_ref[...])
    qk_ref[:, :width] = (y[:, :width] * (1.0 / math.sqrt(SB_HEAD_DIM))).astype(qk_ref.dtype)
    qk_ref[:, width:] = y[:, width:].astype(qk_ref.dtype)
    v_t = _dot_nt(wvt_ref[...], h).astype(vt_ref.dtype)
    for ci in range(vt_ref.shape[0]):
        vt_ref[ci] = v_t[:, ci * SB_BLOCK:(ci + 1) * SB_BLOCK]


def _sb_qkv(x, g, w_qk, w_vt, tm=512):
    t = x.shape[0]
    width = SB_HEADS * SB_HEAD_DIM
    chunks = tm // SB_BLOCK
    return pl.pallas_call(
        _sb_qkv_kernel,
        grid=(t // tm,),
        in_specs=[
            pl.BlockSpec((tm, D_MODEL), lambda i: (i, 0)),
            _const_spec((1, D_MODEL)),
            _const_spec((D_MODEL, 2 * width)),
            _const_spec((width, D_MODEL)),
        ],
        out_specs=[
            pl.BlockSpec((tm, 2 * width), lambda i: (i, 0)),
            pl.BlockSpec((chunks, width, SB_BLOCK), lambda i: (i, 0, 0)),
        ],
        out_shape=[
            jax.ShapeDtypeStruct((t, 2 * width), jnp.bfloat16),
            jax.ShapeDtypeStruct((t // SB_BLOCK, width, SB_BLOCK), jnp.bfloat16),
        ],
        compiler_params=_params(("parallel",)),
        name="sb_qkv",
    )(x, g, w_qk, w_vt)


def _sb_attn_kernel(q_ref, k_ref, v_ref, o_ref):
    qi = pl.program_id(2)
    blk = SB_BLOCK
    heads = range(SB_HEADS_PER_STEP)
    lane = lax.broadcasted_iota(jnp.int32, (blk, LANES), 1)
    key = lax.broadcasted_iota(jnp.int32, (blk, blk), 0)
    qry = lax.broadcasted_iota(jnp.int32, (blk, blk), 1)
    causal = key < qry
    half = blk // 2

    def cum_matrix(tk):
        ur = lax.broadcasted_iota(jnp.int32, (tk + SB_TOTAL_ROWS, 2 * tk), 0)
        uc = lax.broadcasted_iota(jnp.int32, (tk + SB_TOTAL_ROWS, 2 * tk), 1)
        return jnp.where(jnp.logical_or((uc % tk) > ur, ur >= tk), 1.0, 0.0).astype(jnp.bfloat16)

    cum = {blk: cum_matrix(blk), half: cum_matrix(half)}

    def cumulative(tk, hilo):
        both = _dot(cum[tk], hilo)
        return both[:tk], both[tk:tk + 1]
    slabs = [slice((h // 2) * LANES, (h // 2 + 1) * LANES) for h in heads]
    q_heads = []
    for h in heads:
        q = q_ref[0, :, slabs[h]]
        q_heads.append(jnp.where((lane // SB_HEAD_DIM) == h % 2, q, jnp.zeros_like(q)))

    def logits(start, tk, h):
        start = pl.multiple_of(start, tk)
        return _dot_nt(k_ref[0, pl.ds(start, tk), slabs[h]], q_heads[h])

    def softplus_parts(z, mask):
        sp = jnp.maximum(z, 0.0) + jnp.log(1.0 + jnp.exp2(jnp.abs(z) * (-math.log2(math.e))))
        sp_m = sp if mask is None else jnp.where(mask, sp, 0.0)
        hi = sp_m.astype(jnp.bfloat16)
        lo = (sp_m - hi.astype(jnp.float32)).astype(jnp.bfloat16)
        return sp, jnp.concatenate([hi, lo], axis=0)

    def weights(z, sp, inner, tail, mask):
        w = jnp.exp(z - sp - inner - tail)
        return (w if mask is None else jnp.where(mask, w, 0.0)).astype(jnp.bfloat16)

    def values(v_t, a, valid=None):
        if valid is not None:
            v_t = jnp.where(valid, v_t, jnp.zeros_like(v_t))
        return _dot(v_t, a)

    def head_rows(h):
        return slice(h * SB_HEAD_DIM, (h + 1) * SB_HEAD_DIM)

    b1 = jnp.maximum(qi - 1, 0)
    b2 = jnp.maximum(qi - 2, 0)
    spans = [(qi * blk, blk, qi, slice(0, blk), causal, None),
             (b1 * blk, blk, b1, slice(0, blk), None, qi >= 1),
             (b2 * blk + half, half, b2, slice(half, blk), None, qi >= 2)]
    tiles = [(d, h) for d in range(len(spans)) for h in heads]
    z = {(d, h): logits(spans[d][0], spans[d][1], h) for d, h in tiles}
    parts = {t: softplus_parts(z[t], spans[t[0]][4]) for t in tiles}
    sums = {t: cumulative(spans[t[0]][1], parts[t][1]) for t in tiles}
    tails = [jnp.zeros((1, blk), jnp.float32) for _ in heads]
    a = {}
    for d, h in tiles:
        valid = spans[d][5]
        inner, total = sums[d, h]
        a[d, h] = weights(z[d, h], parts[d, h][0], inner, tails[h], spans[d][4])
        tails[h] = tails[h] + (total if valid is None else jnp.where(valid, total, 0.0))
    accs = [sum(values(v_ref[0, chunk, head_rows(h), lanes], a[d, h], valid)
                for d, (_, _, chunk, lanes, _, valid) in enumerate(spans)) for h in heads]

    def live(state):
        smallest = functools.reduce(jnp.minimum, [state[1 + 2 * h] for h in heads])
        return jnp.logical_and(state[0] >= 0, jnp.min(smallest) < SB_TAIL_CUTOFF)

    def older(state):
        hb = state[0]
        newer_half = hb % 2 == 1
        out = [hb - 1]
        for h in heads:
            tail, acc = state[1 + 2 * h], state[2 + 2 * h]
            zz = logits(hb * half, half, h)
            sp, hilo = softplus_parts(zz, None)
            inner, total = cumulative(half, hilo)
            v_blk = v_ref[0, hb // 2, head_rows(h), :]
            v_t = jnp.where(newer_half, v_blk[:, half:], v_blk[:, :half])
            acc = acc + values(v_t, weights(zz, sp, inner, tail, None))
            out += [tail + total, acc]
        return tuple(out)

    init = [2 * qi - 4]
    for h in heads:
        init += [tails[h], accs[h]]
    state = lax.while_loop(live, older, tuple(init))
    o_t = jnp.concatenate([state[2 + 2 * h] for h in heads], axis=0)
    o_ref[0] = o_t.T.astype(o_ref.dtype)


def _sb_attention(qk, v_t, batch, seq):
    groups = SB_HEADS // SB_HEADS_PER_STEP
    width = SB_HEADS_PER_STEP * SB_HEAD_DIM
    nq = seq // SB_BLOCK
    return pl.pallas_call(
        _sb_attn_kernel,
        grid=(batch, groups, nq),
        in_specs=[
            pl.BlockSpec((1, SB_BLOCK, width), lambda b, p, i: (b, i, p)),
            pl.BlockSpec((1, seq, width), lambda b, p, i: (b, 0, groups + p),
                         pipeline_mode=pl.Buffered(1)),
            pl.BlockSpec((1, nq, width, SB_BLOCK), lambda b, p, i: (b, 0, p, 0),
                         pipeline_mode=pl.Buffered(1)),
        ],
        out_specs=pl.BlockSpec((1, SB_BLOCK, width), lambda b, p, i: (b, i, p)),
        out_shape=jax.ShapeDtypeStruct((batch, seq, SB_HEADS * SB_HEAD_DIM), jnp.bfloat16),
        compiler_params=_params(("parallel", "parallel", "arbitrary")),
        name="sb_attn",
    )(qk, qk, v_t)


def _mla_proj_kernel(x_ref, pos_ref, freq_ref, g_ref, win_ref, qn_ref, kvn_ref,
                     wuq_ref, wuk_ref, wuv_ref, q_ref, k_ref, v_ref):
    h = _rms(x_ref[...], g_ref[...]).astype(jnp.bfloat16)
    proj = _dot(h, win_ref[...])
    c_q = proj[:, :MLA_Q_LORA]
    c_kv = proj[:, MLA_Q_LORA:MLA_Q_LORA + MLA_KV_LORA]
    kr = proj[:, MLA_Q_LORA + MLA_KV_LORA:MLA_Q_LORA + MLA_KV_LORA + HEAD_SLAB]
    kr_rot = proj[:, MLA_Q_LORA + MLA_KV_LORA + HEAD_SLAB:]

    tm = x_ref.shape[0]
    lane = lax.broadcasted_iota(jnp.int32, (tm, HEAD_SLAB), 1)
    ang = pos_ref[...].astype(jnp.float32) * freq_ref[...]
    is_rope = jnp.logical_and(lane >= MLA_NOPE_DIM, lane < MLA_NOPE_DIM + MLA_ROPE_DIM)
    cos = jnp.where(lane < MLA_NOPE_DIM, 1.0, jnp.where(is_rope, jnp.cos(ang), 0.0))
    sin = jnp.where(is_rope, jnp.sin(ang), 0.0)

    qn = _rms(c_q, qn_ref[...]).astype(jnp.bfloat16)
    q_all = _dot(qn, wuq_ref[...])
    kvn = _rms(c_kv, kvn_ref[...]).astype(jnp.bfloat16)
    k_nope = _dot(kvn, wuk_ref[...])
    v_t = _dot_nt(wuv_ref[...], kvn).astype(v_ref.dtype)
    for ci in range(v_ref.shape[0]):
        v_ref[ci] = v_t[:, ci * MLA_V_CHUNK:(ci + 1) * MLA_V_CHUNK]
    k_rope = kr * cos + kr_rot * sin
    cos_q = cos * MLA_SCORE_SCALE
    sin_q = sin * MLA_SCORE_SCALE
    width = MLA_HEADS * HEAD_SLAB
    for hd in range(MLA_HEADS):
        sl = slice(hd * HEAD_SLAB, (hd + 1) * HEAD_SLAB)
        rot = slice(width + hd * HEAD_SLAB, width + (hd + 1) * HEAD_SLAB)
        q_ref[:, sl] = (q_all[:, sl] * cos_q + q_all[:, rot] * sin_q).astype(q_ref.dtype)
        k_ref[:, sl] = (k_nope[:, sl] + k_rope).astype(k_ref.dtype)


def _mla_weights(w_in, w_uq, w_ukv):
    half = MLA_ROPE_DIM // 2
    pad = HEAD_SLAB - MLA_NOPE_DIM - MLA_ROPE_DIM

    def slab(nope, rope):
        return jnp.concatenate(
            [nope, rope, jnp.zeros(rope.shape[:-1] + (pad,), rope.dtype)], axis=-1)

    def rotate(rope):
        return jnp.concatenate([-rope[..., half:], rope[..., :half]], axis=-1)

    lat = MLA_Q_LORA + MLA_KV_LORA
    kr_w = w_in[:, lat:]
    zeros_in = jnp.zeros((D_MODEL, MLA_NOPE_DIM), w_in.dtype)
    w_in_p = jnp.concatenate(
        [w_in[:, :lat], slab(zeros_in, kr_w), slab(zeros_in, rotate(kr_w))], axis=1)

    uq = w_uq.reshape(MLA_Q_LORA, MLA_HEADS, MLA_NOPE_DIM + MLA_ROPE_DIM)
    uq_nope, uq_rope = uq[..., :MLA_NOPE_DIM], uq[..., MLA_NOPE_DIM:]
    w_uq_p = jnp.concatenate(
        [slab(uq_nope, uq_rope).reshape(MLA_Q_LORA, -1),
         slab(jnp.zeros_like(uq_nope), rotate(uq_rope)).reshape(MLA_Q_LORA, -1)], axis=1)

    ukv = w_ukv.reshape(MLA_KV_LORA, MLA_HEADS, MLA_NOPE_DIM + MLA_V_DIM)
    uk = ukv[..., :MLA_NOPE_DIM]
    w_uk_p = jnp.concatenate([uk, jnp.zeros_like(uk)], axis=-1).reshape(MLA_KV_LORA, -1)
    w_uv = ukv[..., MLA_NOPE_DIM:].reshape(MLA_KV_LORA, -1).T
    bf = jnp.bfloat16
    return w_in_p.astype(bf), w_uq_p.astype(bf), w_uk_p.astype(bf), w_uv.astype(bf)


def _rope_freq_slab():
    inv_freq = ROPE_THETA ** (-jnp.arange(0, MLA_ROPE_DIM, 2, dtype=jnp.float32) / MLA_ROPE_DIM)
    zeros = jnp.zeros((MLA_NOPE_DIM,), jnp.float32)
    pad = jnp.zeros((HEAD_SLAB - MLA_NOPE_DIM - MLA_ROPE_DIM,), jnp.float32)
    return jnp.concatenate([zeros, inv_freq, inv_freq, pad])[None, :]


def _mla_proj(x, pos, g, w_in_p, q_norm, kv_norm, w_uq_p, w_uk_p, w_uv):
    tm = MLA_PROJ_TOKENS
    chunks = tm // MLA_V_CHUNK
    t = x.shape[0]
    width = MLA_HEADS * HEAD_SLAB
    vwidth = MLA_HEADS * MLA_V_DIM
    row = lambda i: (i, 0)
    return pl.pallas_call(
        _mla_proj_kernel,
        grid=(t // tm,),
        in_specs=[
            pl.BlockSpec((tm, D_MODEL), row),
            pl.BlockSpec((tm, 1), row),
            _const_spec((1, HEAD_SLAB)),
            _const_spec((1, D_MODEL)),
            _const_spec(w_in_p.shape),
            _const_spec((1, MLA_Q_LORA)),
            _const_spec((1, MLA_KV_LORA)),
            _const_spec(w_uq_p.shape),
            _const_spec(w_uk_p.shape),
            _const_spec(w_uv.shape),
        ],
        out_specs=[
            pl.BlockSpec((tm, width), row),
            pl.BlockSpec((tm, width), row),
            pl.BlockSpec((chunks, vwidth, MLA_V_CHUNK), lambda i: (i, 0, 0)),
        ],
        out_shape=[
            jax.ShapeDtypeStruct((t, width), jnp.bfloat16),
            jax.ShapeDtypeStruct((t, width), jnp.bfloat16),
            jax.ShapeDtypeStruct((t // MLA_V_CHUNK, vwidth, MLA_V_CHUNK), jnp.bfloat16),
        ],
        compiler_params=_params(("parallel",)),
        name="mla_proj",
    )(x, pos, _rope_freq_slab(), g, w_in_p, q_norm, kv_norm, w_uq_p, w_uk_p, w_uv)


def _mla_attn_kernel(q_ref, k_ref, v_ref, o_ref,
                     s0_ref, s1_ref, p0_ref, p1_ref, mx_ref, m_ref, alpha_ref, acc_ref):
    qi = pl.program_id(2)
    tq, tk = MLA_Q_BLOCK, MLA_K_BLOCK
    chunks = tk // MLA_V_CHUNK
    heads = range(HEADS_PER_STEP)
    q_heads = [q_ref[0, :, h * HEAD_SLAB:(h + 1) * HEAD_SLAB] for h in heads]
    ones = jnp.ones((MLA_DENOM_ROWS, tk), jnp.bfloat16)
    ratio = tq // tk
    n_full = qi * ratio
    s_bufs = (s0_ref, s1_ref)
    p_bufs = (p0_ref, p1_ref)

    def scores(j, parity, first=0):
        start = pl.multiple_of(j * tk, tk)
        for h in heads:
            k_blk = k_ref[0, pl.ds(start, tk), h * HEAD_SLAB:(h + 1) * HEAD_SLAB]
            s = _dot_nt(k_blk, q_heads[h][first:])
            s_bufs[parity][h, :, first:tq] = s
            mx_ref[parity, h, :, first:] = jnp.max(s, axis=0, keepdims=True)

    def values(j, parity, first=0):
        for h in heads:
            v_t = jnp.concatenate(
                [v_ref[0, j * chunks + ci, h * MLA_V_DIM:(h + 1) * MLA_V_DIM, :]
                 for ci in range(chunks)], axis=1)
            pv = _dot(jnp.concatenate([v_t, ones], axis=0), p_bufs[parity][h, :, first:tq])
            acc_ref[h, :, first:] = alpha_ref[parity, h, :, first:] * acc_ref[h, :, first:] + pv

    def softmax(parity, mask, first=0, last=tq):
        cols = slice(first, last)
        for h in heads:
            s = s_bufs[parity][h, :, cols]
            if mask is None:
                block_max = mx_ref[parity, h, :, cols]
            else:
                s = jnp.where(mask, s, -jnp.inf)
                block_max = jnp.max(s, axis=0, keepdims=True)
            m_old = m_ref[h, :, cols]
            m_new = jnp.maximum(m_old, block_max)
            p_bufs[parity][h, :, cols] = jnp.exp2(s - m_new).astype(jnp.bfloat16)
            alpha_ref[parity, h, :, cols] = jnp.exp2(m_old - m_new)
            m_ref[h, :, cols] = m_new

    def step(i, parity):
        softmax(parity, None)
        scores(i + 1, 1 - parity)
        values(jnp.maximum(i - 1, 0), 1 - parity)

    def diagonal_steps():
        key = lax.broadcasted_iota(jnp.int32, (tk, tk), 0)
        qry = lax.broadcasted_iota(jnp.int32, (tk, tk), 1)
        for off in range(ratio):
            if off + 1 < ratio:
                scores(n_full + off + 1, (off + 1) % 2, (off + 1) * tk)
            softmax(off % 2, key <= qry, off * tk, (off + 1) * tk)
            if off + 1 < ratio:
                softmax(off % 2, None, (off + 1) * tk)
            values(jnp.maximum(n_full + off - 1, 0), (off + 1) % 2, max(off - 1, 0) * tk)
        values(n_full + ratio - 1, (ratio - 1) % 2, (ratio - 1) * tk)
        o_t = jnp.concatenate(
            [acc_ref[h, :MLA_V_DIM] / acc_ref[h, MLA_V_DIM:MLA_V_DIM + 1] for h in heads], axis=0)
        o_ref[0] = o_t.T.astype(o_ref.dtype)

    m_ref[...] = jnp.full(m_ref.shape, -jnp.inf, jnp.float32)
    alpha_ref[...] = jnp.ones(alpha_ref.shape, jnp.float32)
    acc_ref[...] = jnp.zeros(acc_ref.shape, jnp.float32)

    p1_ref[...] = jnp.zeros(p1_ref.shape, jnp.bfloat16)
    scores(0, 0)

    def two_steps(ii, carry):
        step(2 * ii, 0)
        step(2 * ii + 1, 1)
        return carry

    lax.fori_loop(0, n_full // 2, two_steps, 0)
    diagonal_steps()


def _mla_attention(q, k, v, batch, seq):
    assert MLA_Q_BLOCK % (2 * MLA_K_BLOCK) == 0
    assert seq % MLA_Q_BLOCK == 0 and MLA_K_BLOCK % MLA_V_CHUNK == 0
    pairs = MLA_HEADS // HEADS_PER_STEP
    nq = seq // MLA_Q_BLOCK
    qk_w = HEADS_PER_STEP * HEAD_SLAB
    return pl.pallas_call(
        _mla_attn_kernel,
        grid=(batch, pairs, nq),
        in_specs=[
            pl.BlockSpec((1, MLA_Q_BLOCK, qk_w), lambda b, p, i: (b, i, p)),
            pl.BlockSpec((1, seq, qk_w), lambda b, p, i: (b, 0, p), pipeline_mode=pl.Buffered(1)),
            pl.BlockSpec((1, seq // MLA_V_CHUNK, HEADS_PER_STEP * MLA_V_DIM, MLA_V_CHUNK),
                         lambda b, p, i: (b, 0, p, 0), pipeline_mode=pl.Buffered(1)),
        ],
        out_specs=pl.BlockSpec((1, MLA_Q_BLOCK, HEADS_PER_STEP * MLA_V_DIM),
                               lambda b, p, i: (b, i, p)),
        out_shape=jax.ShapeDtypeStruct((batch, seq, MLA_HEADS * MLA_V_DIM), jnp.bfloat16),
        scratch_shapes=[
            pltpu.VMEM((HEADS_PER_STEP, MLA_K_BLOCK, MLA_Q_BLOCK), jnp.float32),
            pltpu.VMEM((HEADS_PER_STEP, MLA_K_BLOCK, MLA_Q_BLOCK), jnp.float32),
            pltpu.VMEM((HEADS_PER_STEP, MLA_K_BLOCK, MLA_Q_BLOCK), jnp.bfloat16),
            pltpu.VMEM((HEADS_PER_STEP, MLA_K_BLOCK, MLA_Q_BLOCK), jnp.bfloat16),
            pltpu.VMEM((2, HEADS_PER_STEP, 1, MLA_Q_BLOCK), jnp.float32),
            pltpu.VMEM((HEADS_PER_STEP, 1, MLA_Q_BLOCK), jnp.float32),
            pltpu.VMEM((2, HEADS_PER_STEP, 1, MLA_Q_BLOCK), jnp.float32),
            pltpu.VMEM((HEADS_PER_STEP, MLA_V_DIM + MLA_DENOM_ROWS, MLA_Q_BLOCK), jnp.float32),
        ],
        compiler_params=_params(("parallel", "parallel", "arbitrary")),
        name="mla_attn",
    )(q, k, v)


def kernel(x, positions, norm_g, ffn_w_in, ffn_w_out, sb_w_in, sb_w_out, mla_w_in,
           mla_q_norm, mla_w_uq, mla_kv_norm, mla_w_ukv, mla_w_out):
    batch, seq, _ = x.shape
    t = batch * seq
    bf = jnp.bfloat16
    depth = norm_g.shape[0]
    xt = x.reshape(t, D_MODEL)
    pos = positions.reshape(t, 1)
    ffn_in = ffn_w_in.astype(bf)
    ffn_out = ffn_w_out.astype(bf)
    for i in range(depth):
        g = norm_g[i][:, None, :]
        xt = _ffn(xt, g[0], g[1], ffn_in, ffn_out, i, 0)
        j = i // 2
        if i % 2 == 0:
            width = SB_HEADS * SB_HEAD_DIM
            w_qk = sb_w_in[j][:, :2 * width].astype(bf)
            w_vt = sb_w_in[j][:, 2 * width:].T.astype(bf)
            qk, v_t = _sb_qkv(xt, g[2], w_qk, w_vt)
            o = _sb_attention(qk.reshape(batch, seq, -1),
                              v_t.reshape(batch, seq // SB_BLOCK, width, SB_BLOCK), batch, seq)
            w_out = sb_w_out[j]
        else:
            w_in_p, w_uq_p, w_uk_p, w_uv = _mla_weights(mla_w_in[j], mla_w_uq[j], mla_w_ukv[j])
            q, k, v = _mla_proj(xt, pos, g[2], w_in_p, mla_q_norm[j][None, :],
                                mla_kv_norm[j][None, :], w_uq_p, w_uk_p, w_uv)
            v_t = v.reshape(batch, seq // MLA_V_CHUNK, MLA_HEADS * MLA_V_DIM, MLA_V_CHUNK)
            o = _mla_attention(q.reshape(batch, seq, -1), k.reshape(batch, seq, -1),
                               v_t, batch, seq)
            w_out = mla_w_out[j]
        xt = _mixer_out_ffn(xt, o.reshape(t, -1), w_out.astype(bf), g[3], g[4], g[5],
                            ffn_in, ffn_out, i, 1)
    return xt.reshape(batch, seq, D_MODEL)
```
